```python
import math
import jax, jax.numpy as jnp
from jax import lax
import numpy as np

D_MODEL = 2048
BATCH = 1
SEQ = 8192
DEPTH = 2

N_META = 16
MIX_WIDTH = D_MODEL
HEAD_SIZE = 64
RWKV_DIM = MIX_WIDTH // 2
RWKV_HEADS = RWKV_DIM // HEAD_SIZE
CONV_DIM = MIX_WIDTH - RWKV_DIM
N_DIR = 2
DECAY_LORA = 64
ICLR_LORA = 64
GATE_LORA = 128
RWKV_COLS = 3 * RWKV_DIM + N_DIR * DECAY_LORA + N_DIR * ICLR_LORA + GATE_LORA
CONV_COLS = 3 * CONV_DIM
IN_COLS = RWKV_COLS + CONV_COLS
CONV_WIDTH = 3
N_EXPERTS = 32
TOP_K = 4
D_EXPERT = D_MODEL
SWIGLU_LIMIT = 7.0
SWIGLU_ALPHA = 1.702
MOE_BLOCK = 128
ALPHA_RES = (2 * DEPTH) ** 0.25
BETA_INIT = (8 * DEPTH) ** -0.25
LN_EPS = 1e-5
GN_EPS = 64e-5

kernel_name = "hybrid_rwkv7_shortconv_moe_encoder"


def layer_norm(x, g, b):
    xf = x.astype(jnp.float32)
    mu = jnp.mean(xf, axis=-1, keepdims=True)
    var = jnp.mean(jnp.square(xf - mu), axis=-1, keepdims=True)
    y = (xf - mu) * lax.rsqrt(var + LN_EPS) * g.astype(jnp.float32) + b.astype(jnp.float32)
    return y.astype(x.dtype)


def shift_prev(u):
    return jnp.pad(u, ((0, 0), (1, 0), (0, 0)))[:, :-1]


def shift_next(u):
    return jnp.pad(u, ((0, 0), (0, 1), (0, 0)))[:, 1:]


def rwkv7_bidir_scan(r, decay, k, v, a_vec, b_vec):
    xs = tuple(jnp.moveaxis(t, 1, 0) for t in (r, decay, k, v, a_vec, b_vec))
    bsz = r.shape[0]
    s0 = jnp.zeros((bsz, N_DIR, RWKV_HEADS, HEAD_SIZE, HEAD_SIZE), jnp.float32)

    def step(S, inp):
        r_t, w_t, k_t, v_t, a_t, b_t = inp
        sa = jnp.einsum('bdhij,bdhj->bdhi', S, a_t)
        S = S * w_t[..., None, :] + sa[..., :, None] * b_t[..., None, :] + v_t[..., :, None] * k_t[..., None, :]
        y = jnp.einsum('bdhij,bdhj->bdhi', S, r_t)
        return S, y

    _, ys = lax.scan(step, s0, xs)
    return jnp.moveaxis(ys, 0, 1)


def mixer(h, w_in, mu_shift, decay0, decay_up, iclr0, iclr_up, gate_up, k_k, k_a, r_k,
          lnx_g, lnx_b, conv_w, out_scale, w_out):
    bsz, seqlen = h.shape[0], h.shape[1]
    p = h @ w_in

    R = p[..., :RWKV_COLS].astype(jnp.float32)
    R = R + (0.5 * (shift_prev(R) + shift_next(R)) - R) * mu_shift
    r = R[..., :RWKV_DIM]
    k = R[..., RWKV_DIM:2 * RWKV_DIM]
    v = R[..., 2 * RWKV_DIM:3 * RWKV_DIM]
    o = 3 * RWKV_DIM
    cw = R[..., o:o + N_DIR * DECAY_LORA].reshape(bsz, seqlen, N_DIR, DECAY_LORA)
    o += N_DIR * DECAY_LORA
    ca = R[..., o:o + N_DIR * ICLR_LORA].reshape(bsz, seqlen, N_DIR, ICLR_LORA)
    o += N_DIR * ICLR_LORA
    cg = R[..., o:o + GATE_LORA]

    w_log = -jax.nn.softplus(-(decay0 + jnp.einsum('bldr,drc->bldc', jnp.tanh(cw), decay_up))) - 0.5
    decay = jnp.exp(-jnp.exp(w_log))
    a = jax.nn.sigmoid(iclr0 + jnp.einsum('bldr,drc->bldc', ca, iclr_up))
    g = jax.nn.sigmoid(cg) @ gate_up

    heads = lambda t: t.reshape(t.shape[:-1] + (RWKV_HEADS, HEAD_SIZE))
    kk = heads(k * k_k)
    kk = kk / jnp.maximum(jnp.linalg.norm(kk, axis=-1, keepdims=True), 1e-12)
    kk = kk.reshape(bsz, seqlen, RWKV_DIM)
    k_dir = k[:, :, None, :] * (1.0 + (a - 1.0) * k_a)
    b_dir = kk[:, :, None, :] * a

    def to_dirs(f, bw):
        return heads(jnp.stack([f, jnp.flip(bw, axis=1)], axis=2))

    ys = rwkv7_bidir_scan(
        to_dirs(r, r), to_dirs(decay[:, :, 0], decay[:, :, 1]),
        to_dirs(k_dir[:, :, 0], k_dir[:, :, 1]), to_dirs(v, v),
        to_dirs(-kk, -kk), to_dirs(b_dir[:, :, 0], b_dir[:, :, 1]))
    y = ys[:, :, 0] + jnp.flip(ys[:, :, 1], axis=1)

    mu = jnp.mean(y, axis=-1, keepdims=True)
    var = jnp.mean(jnp.square(y - mu), axis=-1, keepdims=True)
    y = (y - mu) * lax.rsqrt(var + GN_EPS) * heads(lnx_g) + heads(lnx_b)
    rh, kh, vh = heads(r), heads(k), heads(v)
    y = y + jnp.sum(rh * kh * r_k, axis=-1, keepdims=True) * vh
    o_rwkv = (y.reshape(bsz, seqlen, RWKV_DIM) * g).astype(h.dtype)

    Cv = p[..., RWKV_COLS:]
    gate_b = Cv[..., :CONV_DIM]
    gate_c = Cv[..., CONV_DIM:2 * CONV_DIM]
    hc = Cv[..., 2 * CONV_DIM:]
    u = gate_c * hc
    u = conv_w[0] * shift_prev(u) + conv_w[1] * u + conv_w[2] * shift_next(u)
    o_conv = gate_b * u

    o_cat = jnp.concatenate([o_rwkv, o_conv], axis=-1) * out_scale
    return o_cat @ w_out


def clamped_swiglu(hid):
    glu = jnp.minimum(hid[..., :D_EXPERT], SWIGLU_LIMIT)
    lin = jnp.clip(hid[..., D_EXPERT:], -SWIGLU_LIMIT, SWIGLU_LIMIT)
    return glu * jax.nn.sigmoid(SWIGLU_ALPHA * glu) * (lin + 1.0)


def moe(x2d, w_router, b_router, w_exp_in, b_exp_in, w_exp_out, b_exp_out):
    T, D = x2d.shape
    logits = (x2d @ w_router + b_router).astype(jnp.float32)
    top_v, top_i = lax.top_k(logits, TOP_K)
    gates = jax.nn.softmax(top_v, axis=-1)

    A = T * TOP_K
    flat_e = top_i.reshape(-1).astype(jnp.int32)
    flat_tok = jnp.arange(A, dtype=jnp.int32) // TOP_K
    flat_g = gates.reshape(-1)
    order = jnp.argsort(flat_e)
    sorted_e = flat_e[order]
    counts = jnp.bincount(flat_e, length=N_EXPERTS)
    starts = jnp.cumsum(counts) - counts
    padded = (counts + MOE_BLOCK - 1) // MOE_BLOCK * MOE_BLOCK
    pends = jnp.cumsum(padded)
    pstarts = pends - padded
    dest = pstarts[sorted_e] + (jnp.arange(A, dtype=jnp.int32) - starts[sorted_e])

    n_blocks = -(-(A + N_EXPERTS * (MOE_BLOCK - 1)) // MOE_BLOCK)
    P = n_blocks * MOE_BLOCK
    row_tok = jnp.full((P,), T, jnp.int32).at[dest].set(flat_tok[order])
    row_gate = jnp.zeros((P,), jnp.float32).at[dest].set(flat_g[order])
    block_expert = jnp.clip(
        jnp.searchsorted(pends, jnp.arange(n_blocks, dtype=jnp.int32) * MOE_BLOCK, side='right'),
        0, N_EXPERTS - 1)

    x_pad = jnp.concatenate([x2d, jnp.zeros((1, D), x2d.dtype)], axis=0)
    xb = x_pad[row_tok].reshape(n_blocks, MOE_BLOCK, D)

    def expert_block(args):
        xblk, e = args
        hid = xblk @ w_exp_in[e] + b_exp_in[e]
        return clamped_swiglu(hid) @ w_exp_out[e] + b_exp_out[e]

    yb = lax.map(expert_block, (xb, block_expert))
    y_rows = yb.reshape(P, D).astype(jnp.float32) * row_gate[:, None]
    out = jax.ops.segment_sum(y_rows, row_tok, num_segments=T + 1)[:T]
    return out.astype(x2d.dtype)


def setup_inputs(seed: int = 0) -> dict:
    key = jax.random.key(seed)
    ks = jax.random.split(key, 32)
    f32 = jnp.float32
    nrm = lambda k, shape, s: jax.random.normal(k, shape, f32) * s
    L_ = DEPTH
    return {
        "x": nrm(ks[0], (BATCH, SEQ, D_MODEL), 1.0),
        "meta_tokens": nrm(ks[1], (N_META, D_MODEL), 1.0),
        "ln_in_g": 1.0 + nrm(ks[2], (D_MODEL,), 0.02),
        "ln_in_b": nrm(ks[3], (D_MODEL,), 0.02),
        "w_in": nrm(ks[4], (L_, D_MODEL, IN_COLS), D_MODEL ** -0.5),
        "mu_shift": jax.random.uniform(ks[5], (L_, RWKV_COLS), f32),
        "decay0": jax.random.uniform(ks[6], (L_, N_DIR, RWKV_DIM), f32, -6.0, -1.0),
        "decay_up": nrm(ks[7], (L_, N_DIR, DECAY_LORA, RWKV_DIM), 0.1 * DECAY_LORA ** -0.5),
        "iclr0": nrm(ks[8], (L_, N_DIR, RWKV_DIM), 0.5),
        "iclr_up": nrm(ks[9], (L_, N_DIR, ICLR_LORA, RWKV_DIM), 0.5 * ICLR_LORA ** -0.5),
        "gate_up": nrm(ks[10], (L_, GATE_LORA, RWKV_DIM), GATE_LORA ** -0.5),
        "k_k": 0.85 + nrm(ks[11], (L_, RWKV_DIM), 0.02),
        "k_a": 1.0 + nrm(ks[12], (L_, RWKV_DIM), 0.02),
        "r_k": nrm(ks[13], (L_, RWKV_HEADS, HEAD_SIZE), 0.1),
        "lnx_g": 1.0 + nrm(ks[14], (L_, RWKV_DIM), 0.02),
        "lnx_b": nrm(ks[15], (L_, RWKV_DIM), 0.02),
        "conv_w": nrm(ks[16], (L_, CONV_WIDTH, CONV_DIM), CONV_WIDTH ** -0.5),
        "out_scale": 1.0 + nrm(ks[17], (L_, MIX_WIDTH), 0.02),
        "w_out": nrm(ks[18], (L_, MIX_WIDTH, D_MODEL), BETA_INIT * MIX_WIDTH ** -0.5),
        "ln1_g": 1.0 + nrm(ks[19], (L_, D_MODEL), 0.02),
        "ln1_b": nrm(ks[20], (L_, D_MODEL), 0.02),
        "w_router": nrm(ks[21], (L_, D_MODEL, N_EXPERTS), D_MODEL ** -0.5),
        "b_router": nrm(ks[22], (L_, N_EXPERTS), 0.01),
        "w_exp_in": nrm(ks[23], (L_, N_EXPERTS, D_MODEL, 2 * D_EXPERT), D_MODEL ** -0.5),
        "b_exp_in": nrm(ks[24], (L_, N_EXPERTS, 2 * D_EXPERT), 0.02),
        "w_exp_out": nrm(ks[25], (L_, N_EXPERTS, D_EXPERT, D_MODEL), BETA_INIT * D_EXPERT ** -0.5),
        "b_exp_out": nrm(ks[26], (L_, N_EXPERTS, D_MODEL), 0.02),
        "ln2_g": 1.0 + nrm(ks[27], (L_, D_MODEL), 0.02),
        "ln2_b": nrm(ks[28], (L_, D_MODEL), 0.02),
    }


def reference(x, meta_tokens, ln_in_g, ln_in_b, w_in, mu_shift, decay0, decay_up, iclr0, iclr_up,
              gate_up, k_k, k_a, r_k, lnx_g, lnx_b, conv_w, out_scale, w_out, ln1_g, ln1_b,
              w_router, b_router, w_exp_in, b_exp_in, w_exp_out, b_exp_out, ln2_g, ln2_b):
    bsz = x.shape[0]
    meta = jnp.broadcast_to(meta_tokens[None].astype(x.dtype), (bsz, N_META, D_MODEL))
    h = jnp.concatenate([meta, x], axis=1)
    h = layer_norm(h, ln_in_g, ln_in_b)
    for l in range(DEPTH):
        m = mixer(h, w_in[l], mu_shift[l], decay0[l], decay_up[l], iclr0[l], iclr_up[l], gate_up[l],
                  k_k[l], k_a[l], r_k[l], lnx_g[l], lnx_b[l], conv_w[l], out_scale[l], w_out[l])
        h = layer_norm(ALPHA_RES * h + m, ln1_g[l], ln1_b[l])
        f = moe(h.reshape(-1, D_MODEL), w_router[l], b_router[l], w_exp_in[l], b_exp_in[l],
                w_exp_out[l], b_exp_out[l]).reshape(h.shape)
        h = layer_norm(ALPHA_RES * h + f, ln2_g[l], ln2_b[l])
    return h[:, N_META:]
```

```python
import functools

import jax
import jax.numpy as jnp
from jax import lax
from jax.experimental import pallas as pl
from jax.experimental.pallas import tpu as pltpu

F32 = jnp.float32
BF16 = jnp.bfloat16
HI = lax.Precision.HIGHEST

D_MODEL = 2048
N_META = 16
HEAD = 64
RW = 1024
CONV = 1024
LORA = 128
RW_COLS = 3 * RW + 3 * LORA
CONV_COLS = 3 * CONV
N_EXP = 32
TOP_K = 4
D_EXP = 2048
MOE_BLOCK = 128
SWIGLU_LIMIT = 7.0
SWIGLU_ALPHA = 1.702
DEPTH = 2
ALPHA_RES = (2 * DEPTH) ** 0.25
LN_EPS = 1e-5
GN_EPS = 64e-5

LANES = 128
SUBLANES = 8
VMEM_LIMIT = 56 * 1024 * 1024

CHUNK = 16
PAIRS = RW // LANES
MOE_TMX = 1024
MOE_SUB = 256
MOE_TF = 512


def _largest_divisor(n, multiple, cap):
    best = None
    for d in range(multiple, cap + 1, multiple):
        if n % d == 0:
            best = d
    assert best is not None, (n, multiple, cap)
    return best


def _cparams(sem):
    return pltpu.CompilerParams(dimension_semantics=sem, vmem_limit_bytes=VMEM_LIMIT)


def _layer_norm(x, g, b):
    mu = jnp.mean(x, axis=-1, keepdims=True)
    xc = x - mu
    var = jnp.mean(xc * xc, axis=-1, keepdims=True)
    return xc * lax.rsqrt(var + LN_EPS) * g + b


def _head_ones():
    r = lax.broadcasted_iota(jnp.int32, (LANES, LANES), 0) >> 6
    c = lax.broadcasted_iota(jnp.int32, (LANES, LANES), 1) >> 6
    return (r == c).astype(F32)


def _head_sum(x, ones):
    return jnp.dot(x, ones, precision=HI, preferred_element_type=F32)


def _ln_in_kernel(x_ref, g_ref, b_ref, h_ref, hb_ref):
    y = _layer_norm(x_ref[...], g_ref[...], b_ref[...])
    h_ref[...] = y
    hb_ref[...] = y.astype(BF16)


def _ln_in(x, g, b, tm):
    t, d = x.shape
    row = pl.BlockSpec((tm, d), lambda i: (i, 0))
    vec = pl.BlockSpec((1, d), lambda i: (0, 0))
    return pl.pallas_call(
        _ln_in_kernel,
        grid=(t // tm,),
        in_specs=[row, vec, vec],
        out_specs=[row, row],
        out_shape=[jax.ShapeDtypeStruct((t, d), F32), jax.ShapeDtypeStruct((t, d), BF16)],
        compiler_params=_cparams(("parallel",)),
        name="ln_in",
    )(x, g.reshape(1, d), b.reshape(1, d))


def _mm_kernel(a_ref, w_ref, o_ref):
    o_ref[...] = jnp.dot(a_ref[...], w_ref[...], preferred_element_type=F32)


def _matmul(a, w, tm, tn):
    t, k = a.shape
    n = w.shape[1]
    return pl.pallas_call(
        _mm_kernel,
        grid=(n // tn, t // tm),
        in_specs=[pl.BlockSpec((tm, k), lambda j, i: (i, 0)),
                  pl.BlockSpec((k, tn), lambda j, i: (0, j))],
        out_specs=pl.BlockSpec((tm, tn), lambda j, i: (i, j)),
        out_shape=jax.ShapeDtypeStruct((t, n), F32),
        compiler_params=_cparams(("parallel", "parallel")),
        name="in_proj",
    )(a, w)


def _shift_rows(x, prev_row, next_row):
    tm = x.shape[0]
    rows = lax.broadcasted_iota(jnp.int32, x.shape, 0)
    xp = jnp.where(rows == 0, prev_row, pltpu.roll(x, 1, axis=0))
    xn = jnp.where(rows == tm - 1, next_row, pltpu.roll(x, tm - 1, axis=0))
    return xp, xn


def _halo_specs(tm, cols, nblk):
    per = tm // SUBLANES
    main = pl.BlockSpec((tm, cols), lambda i: (i, 0))
    prev = pl.BlockSpec((SUBLANES, cols), lambda i: (jnp.maximum(i * per - 1, 0), 0))
    nxt = pl.BlockSpec((SUBLANES, cols), lambda i: (jnp.minimum((i + 1) * per, nblk * per - 1), 0))
    return main, prev, nxt


def _prep_kernel(p_ref, pp_ref, pn_ref, mu_ref, d0_ref, du_ref, i0_ref, iu_ref, gu_ref,
                 kk_ref, ka_ref, rk_ref,
                 r_ref, v_ref, a_ref, w_ref, iw_ref, kd_ref, bd_ref, bonus_ref, g_ref):
    i = pl.program_id(0)
    first = i == 0
    last = i == pl.num_programs(0) - 1
    ones = _head_ones()

    def mixed(c0):
        cs = slice(c0, c0 + LANES)
        x = p_ref[:, cs]
        prev_row = jnp.where(first, 0.0, pp_ref[SUBLANES - 1:SUBLANES, cs])
        next_row = jnp.where(last, 0.0, pn_ref[0:1, cs])
        xp, xn = _shift_rows(x, prev_row, next_row)
        return x + (0.5 * (xp + xn) - x) * mu_ref[:, cs]

    tanh_cw = jnp.tanh(mixed(3 * RW))
    ca = mixed(3 * RW + LORA)
    sig_cg = jax.nn.sigmoid(mixed(3 * RW + 2 * LORA))

    for s in range(RW // LANES):
        cs = slice(s * LANES, (s + 1) * LANES)
        r = mixed(s * LANES)
        k = mixed(RW + s * LANES)
        v = mixed(2 * RW + s * LANES)
        kk = k * kk_ref[:, cs]
        norm = jnp.sqrt(_head_sum(kk * kk, ones))
        kk = kk / jnp.maximum(norm, 1e-12)
        r_ref[:, cs] = r
        v_ref[:, cs] = v
        a_ref[:, cs] = -kk
        bonus_ref[:, cs] = _head_sum(r * k * rk_ref[:, cs], ones) * v
        g_ref[:, cs] = jnp.dot(sig_cg, gu_ref[:, cs], precision=HI, preferred_element_type=F32)
        for d in range(2):
            ds_ = slice(d * RW + s * LANES, d * RW + (s + 1) * LANES)
            z = d0_ref[:, ds_] + jnp.dot(tanh_cw, du_ref[:, ds_], precision=HI,
                                         preferred_element_type=F32)
            w_log = -(jnp.maximum(-z, 0.0) + jnp.log(1.0 + jnp.exp(-jnp.abs(z)))) - 0.5
            e1 = jnp.exp(w_log)
            w_ref[d, :, cs] = jnp.exp(-e1)
            iw_ref[d, :, cs] = jnp.exp(e1)
            a_icl = jax.nn.sigmoid(i0_ref[:, ds_] + jnp.dot(ca, iu_ref[:, ds_], precision=HI,
                                                            preferred_element_type=F32))
            kd_ref[d, :, cs] = k * (1.0 + (a_icl - 1.0) * ka_ref[:, cs])
            bd_ref[d, :, cs] = kk * a_icl


def _block_diag2(m):
    z = jnp.zeros_like(m[0])
    return jnp.concatenate([jnp.concatenate([m[0], z], axis=1),
                            jnp.concatenate([z, m[1]], axis=1)], axis=0)


def _prep(pr, mu, decay0, decay_up, iclr0, iclr_up, gate_up, k_k, k_a, r_k, tm):
    t = pr.shape[0]
    nblk = t // tm
    main, prev, nxt = _halo_specs(tm, RW_COLS, nblk)

    def full(shape):
        return pl.BlockSpec(shape, lambda i: (0,) * len(shape))

    row = pl.BlockSpec((tm, RW), lambda i: (i, 0))
    row2 = pl.BlockSpec((2, tm, RW), lambda i: (0, i, 0))
    one = jax.ShapeDtypeStruct((t, RW), F32)
    two = jax.ShapeDtypeStruct((2, t, RW), F32)
    return pl.pallas_call(
        _prep_kernel,
        grid=(nblk,),
        in_specs=[main, prev, nxt, full((1, RW_COLS)), full((1, 2 * RW)), full((LORA, 2 * RW)),
                  full((1, 2 * RW)), full((LORA, 2 * RW)), full((LORA, RW)),
                  full((1, RW)), full((1, RW)), full((1, RW))],
        out_specs=[row, row, row, row2, row2, row2, row2, row, row],
        out_shape=[one, one, one, two, two, two, two, one, one],
        compiler_params=_cparams(("parallel",)),
        name="rwkv_prep",
    )(pr, pr, pr, mu.reshape(1, RW_COLS), decay0.reshape(1, 2 * RW), _block_diag2(decay_up),
      iclr0.reshape(1, 2 * RW), _block_diag2(iclr_up), gate_up,
      k_k.reshape(1, RW), k_a.reshape(1, RW), r_k.reshape(1, RW))


def _scan_products():
    prods = []
    for t in range(CHUNK):
        for s in range(t):
            prods.append(("ab", t, s))
    for t in range(CHUNK):
        for s in range(t):
            prods.append(("ak", t, s))
    for t in range(CHUNK):
        for s in range(t + 1):
            prods.append(("rb", t, s))
    for t in range(CHUNK):
        for s in range(t + 1):
            prods.append(("rk", t, s))
    return prods


_PRODS = _scan_products()
_PROD_ROW = {p: SUBLANES * n for n, p in enumerate(_PRODS)}
_N_PROD_ROWS = SUBLANES * len(_PRODS)
_PROD_TILE = 512


def _scan_kernel(r_ref, v_ref, a_ref, w_ref, iw_ref, k_ref, b_ref, y_ref,
                 s_ref, at_ref, rt_ref, bt_ref, kt_ref, ut_ref, sa_ref, sr_ref, gc_ref,
                 lhs_ref, m_ref, *, n_chunks):
    d = pl.program_id(0)
    c16 = CHUNK * SUBLANES

    @pl.when(pl.program_id(1) == 0)
    def _():
        s_ref[...] = jnp.zeros_like(s_ref)

    rr = lax.broadcasted_iota(jnp.int32, (2 * LANES, LANES), 0)
    cc = lax.broadcasted_iota(jnp.int32, (2 * LANES, LANES), 1)
    ones2 = (((rr >> 6) & 1) == (cc >> 6)).astype(BF16)
    diag_mask = _head_ones()

    def chunk_body(c, carry):
        cn = c + d * (n_chunks - 1 - 2 * c)
        base = pl.multiple_of(cn * c16, c16)

        def loc(tau):
            return pl.multiple_of((tau + d * (CHUNK - 1 - 2 * tau)) * SUBLANES, SUBLANES)

        def rows(ref, tau):
            return ref[pl.ds(pl.multiple_of(base + loc(tau), SUBLANES), SUBLANES), :]

        g = None
        ig = None
        for tau in range(CHUNK):
            w = rows(w_ref, tau)
            iw = rows(iw_ref, tau)
            a = rows(a_ref, tau)
            at_ref[pl.ds(loc(tau), SUBLANES), :] = a if g is None else a * g
            g = w if g is None else g * w
            ig = iw if ig is None else ig * iw
            rt_ref[pl.ds(loc(tau), SUBLANES), :] = rows(r_ref, tau) * g
            bt_ref[pl.ds(loc(tau), SUBLANES), :] = rows(b_ref, tau) * ig
            kt_ref[pl.ds(loc(tau), SUBLANES), :] = rows(k_ref, tau) * ig
        gc_ref[...] = g

        for p in range(PAIRS):
            ar = jnp.concatenate([at_ref[pl.ds(p, CHUNK, stride=SUBLANES), :],
                                  rt_ref[pl.ds(p, CHUNK, stride=SUBLANES), :]], axis=0)
            res = lax.dot_general(ar, s_ref[p], (((1,), (1,)), ((), ())), precision=HI,
                                  preferred_element_type=F32)
            sa_ref[pl.ds(p, CHUNK, stride=SUBLANES), :] = res[:CHUNK]
            sr_ref[pl.ds(p, CHUNK, stride=SUBLANES), :] = res[CHUNK:]

        srcs = {"a": at_ref, "r": rt_ref, "b": bt_ref, "k": kt_ref}

        def prod(n):
            kind, t, s = _PRODS[n]
            return (srcs[kind[0]][pl.ds(loc(t), SUBLANES), :]
                    * srcs[kind[1]][pl.ds(loc(s), SUBLANES), :])

        for m in range(len(_PRODS) // 2):
            x = jnp.concatenate([prod(2 * m), prod(2 * m + 1)], axis=0)
            hi = x.astype(BF16)
            lo = (x - hi.astype(F32)).astype(BF16)
            lhs_ref[16 * m:16 * (m + 1), 0:LANES] = hi
            lhs_ref[16 * m:16 * (m + 1), LANES:2 * LANES] = lo
        for q in range(_N_PROD_ROWS // _PROD_TILE):
            sl = slice(q * _PROD_TILE, (q + 1) * _PROD_TILE)
            m_ref[sl, :] = jnp.dot(lhs_ref[sl, :], ones2, preferred_element_type=F32)

        def coef(kind, t, s):
            r0 = _PROD_ROW[(kind, t, s)]
            return m_ref[r0:r0 + SUBLANES, :]

        vs = [rows(v_ref, tau) for tau in range(CHUNK)]
        us = []
        for t in range(CHUNK):
            acc = sa_ref[pl.ds(loc(t), SUBLANES), :]
            for s in range(t):
                acc = acc + coef("ak", t, s) * vs[s]
            for s in range(t):
                acc = acc + coef("ab", t, s) * us[s]
            us.append(acc)
            ut_ref[pl.ds(loc(t), SUBLANES), :] = acc

        for t in range(CHUNK):
            acc = sr_ref[pl.ds(loc(t), SUBLANES), :]
            for s in range(t + 1):
                acc = acc + coef("rk", t, s) * vs[s] + coef("rb", t, s) * us[s]
            y_ref[pl.ds(pl.multiple_of(base + loc(t), SUBLANES), SUBLANES), :] = acc

        for p in range(PAIRS):
            gp = gc_ref[p:p + 1, :]
            bk = jnp.concatenate([bt_ref[pl.ds(p, CHUNK, stride=SUBLANES), :],
                                  kt_ref[pl.ds(p, CHUNK, stride=SUBLANES), :]], axis=0) * gp
            uv = jnp.concatenate([ut_ref[pl.ds(p, CHUNK, stride=SUBLANES), :],
                                  v_ref[pl.ds(pl.multiple_of(base + p, 1), CHUNK, stride=SUBLANES), :]],
                                 axis=0)
            upd = lax.dot_general(uv, bk, (((0,), (0,)), ((), ())), precision=HI,
                                  preferred_element_type=F32)
            s_ref[p] = s_ref[p] * gp + upd * diag_mask
        return carry

    lax.fori_loop(0, n_chunks, chunk_body, 0)


def _scan(r, v, a, w, iw, kd, bd, tb):
    t = r.shape[0]
    nb = t // tb
    rows = tb * SUBLANES

    def blk(d, i):
        return i + d * (nb - 1 - 2 * i)

    shared = pl.BlockSpec((rows, LANES), lambda d, i: (blk(d, i), 0))
    per_dir = pl.BlockSpec((None, rows, LANES), lambda d, i: (d, blk(d, i), 0))
    flat = lambda x: x.reshape(t * SUBLANES, LANES)
    flat2 = lambda x: x.reshape(2, t * SUBLANES, LANES)
    c16 = CHUNK * SUBLANES
    y = pl.pallas_call(
        functools.partial(_scan_kernel, n_chunks=tb // CHUNK),
        grid=(2, nb),
        in_specs=[shared, shared, shared, per_dir, per_dir, per_dir, per_dir],
        out_specs=per_dir,
        out_shape=jax.ShapeDtypeStruct((2, t * SUBLANES, LANES), F32),
        scratch_shapes=[pltpu.VMEM((PAIRS, LANES, LANES), F32)]
        + [pltpu.VMEM((c16, LANES), F32) for _ in range(7)]
        + [pltpu.VMEM((SUBLANES, LANES), F32),
           pltpu.VMEM((_N_PROD_ROWS, 2 * LANES), BF16),
           pltpu.VMEM((_N_PROD_ROWS, LANES), F32)],
        compiler_params=_cparams(("arbitrary", "arbitrary")),
        name="rwkv_scan",
    )(flat(r), flat(v), flat(a), flat2(w), flat2(iw), flat2(kd), flat2(bd))
    return y.reshape(2, t, RW)


def _top4(logits):
    lane = lax.broadcasted_iota(jnp.int32, logits.shape, 1).astype(F32)
    vals, idxs = [], []
    l = logits
    for _ in range(TOP_K):
        m = jnp.max(l, axis=-1, keepdims=True)
        idx = jnp.min(jnp.where(l == m, lane, float(LANES)), axis=-1, keepdims=True)
        vals.append(m)
        idxs.append(idx)
        l = jnp.where(lane == idx, -jnp.inf, l)
    es = [jnp.exp(vk - vals[0]) for vk in vals]
    tot = es[0] + es[1] + es[2] + es[3]
    top_i = jnp.zeros(logits.shape, F32)
    gates = jnp.zeros(logits.shape, F32)
    for k in range(TOP_K):
        top_i = jnp.where(lane == float(k), idxs[k], top_i)
        gates = jnp.where(lane == float(k), es[k] / tot, gates)
    return top_i.astype(jnp.int32), gates


def _post_kernel(y_ref, bonus_ref, g_ref, c_ref, cp_ref, cn_ref, h_ref, wo_ref,
                 lg_ref, lb_ref, cw_ref, os_ref, n1g_ref, n1b_ref, wr_ref, br_ref,
                 h1_ref, h1b_ref, ti_ref, gt_ref, ocat_ref):
    i = pl.program_id(0)
    first = i == 0
    last = i == pl.num_programs(0) - 1
    ones = _head_ones()
    inv_n = 1.0 / HEAD

    for s in range(RW // LANES):
        cs = slice(s * LANES, (s + 1) * LANES)
        y = y_ref[0, :, cs] + y_ref[1, :, cs]
        mu = _head_sum(y, ones) * inv_n
        yc = y - mu
        var = _head_sum(yc * yc, ones) * inv_n
        yn = yc * lax.rsqrt(var + GN_EPS) * lg_ref[:, cs] + lb_ref[:, cs] + bonus_ref[:, cs]
        ocat_ref[:, cs] = (yn * g_ref[:, cs] * os_ref[:, cs]).astype(BF16)

    for s in range(CONV // LANES):
        cs = slice(s * LANES, (s + 1) * LANES)
        gs = slice(CONV + s * LANES, CONV + (s + 1) * LANES)
        hs = slice(2 * CONV + s * LANES, 2 * CONV + (s + 1) * LANES)
        u = c_ref[:, gs] * c_ref[:, hs]
        u_prev = jnp.where(first, 0.0, cp_ref[SUBLANES - 1:SUBLANES, gs] * cp_ref[SUBLANES - 1:SUBLANES, hs])
        u_next = jnp.where(last, 0.0, cn_ref[0:1, gs] * cn_ref[0:1, hs])
        up, un = _shift_rows(u, u_prev, u_next)
        conv = cw_ref[0:1, cs] * up + cw_ref[1:2, cs] * u + cw_ref[2:3, cs] * un
        oc = slice(RW + s * LANES, RW + (s + 1) * LANES)
        ocat_ref[:, oc] = (c_ref[:, cs] * conv * os_ref[:, oc]).astype(BF16)

    m = jnp.dot(ocat_ref[...], wo_ref[...], preferred_element_type=F32)
    h1 = _layer_norm(ALPHA_RES * h_ref[...] + m, n1g_ref[...], n1b_ref[...])
    h1_ref[...] = h1
    h1b_ref[...] = h1.astype(BF16)
    logits = jnp.dot(h1, wr_ref[...], precision=HI, preferred_element_type=F32) + br_ref[...]
    ti, gt = _top4(logits)
    ti_ref[...] = ti
    gt_ref[...] = gt


def _post(y2, bonus, g, pc, h, w_out_b, lnx_g, lnx_b, conv_w, out_scale, ln1_g, ln1_b,
          w_router, b_router, tm):
    t = h.shape[0]
    nblk = t // tm
    main, prev, nxt = _halo_specs(tm, CONV_COLS, nblk)

    def full(shape):
        return pl.BlockSpec(shape, lambda i: (0,) * len(shape))

    row = lambda c: pl.BlockSpec((tm, c), lambda i: (i, 0))
    wr = jnp.zeros((D_MODEL, LANES), F32).at[:, :N_EXP].set(w_router)
    br = jnp.full((1, LANES), -1e30, F32).at[0, :N_EXP].set(b_router)
    return pl.pallas_call(
        _post_kernel,
        grid=(nblk,),
        in_specs=[pl.BlockSpec((2, tm, RW), lambda i: (0, i, 0)), row(RW), row(RW),
                  main, prev, nxt, row(D_MODEL), full((D_MODEL, D_MODEL)),
                  full((1, RW)), full((1, RW)), full((3, CONV)), full((1, D_MODEL)),
                  full((1, D_MODEL)), full((1, D_MODEL)), full((D_MODEL, LANES)), full((1, LANES))],
        out_specs=[row(D_MODEL), row(D_MODEL), row(LANES), row(LANES)],
        out_shape=[jax.ShapeDtypeStruct((t, D_MODEL), F32), jax.ShapeDtypeStruct((t, D_MODEL), BF16),
                   jax.ShapeDtypeStruct((t, LANES), jnp.int32), jax.ShapeDtypeStruct((t, LANES), F32)],
        scratch_shapes=[pltpu.VMEM((tm, D_MODEL), BF16)],
        compiler_params=_cparams(("parallel",)),
        name="mixer_out",
    )(y2, bonus, g, pc, pc, pc, h, w_out_b, lnx_g.reshape(1, RW), lnx_b.reshape(1, RW), conv_w,
      out_scale.reshape(1, D_MODEL), ln1_g.reshape(1, D_MODEL), ln1_b.reshape(1, D_MODEL), wr, br)


def _moe_kernel(ve_ref, vr_ref, vn_ref, tail_ref, x_hbm, wg_ref, wl_ref, bg_ref, bl_ref, wo_ref, bo_ref,
                y_hbm, xbuf, acc, wg_b, wl_b, wo_b, sem_in, sem_out):
    v = pl.program_id(0)
    f = pl.program_id(1)
    n128 = vn_ref[v]
    n_sub = (n128 + 1) >> 1
    row0 = pl.multiple_of(vr_ref[v], MOE_BLOCK)

    def in_copy(j):
        r = pl.multiple_of(j * MOE_SUB, MOE_SUB)
        return pltpu.make_async_copy(x_hbm.at[pl.ds(row0 + r, MOE_SUB), :],
                                     xbuf.at[pl.ds(r, MOE_SUB), :], sem_in)

    def out_copy(j):
        r = pl.multiple_of(j * MOE_BLOCK, MOE_BLOCK)
        return pltpu.make_async_copy(acc.at[pl.ds(r, MOE_BLOCK), :],
                                     y_hbm.at[pl.ds(row0 + r, MOE_BLOCK), :], sem_out)

    @pl.when(jnp.logical_and(f == 0, n128 > 0))
    def _():
        lax.fori_loop(0, n_sub, lambda j, c: (in_copy(j).start(), c)[1], 0)
        lax.fori_loop(0, n_sub, lambda j, c: (in_copy(j).wait(), c)[1], 0)

    @pl.when(n128 > 0)
    def _():
        wg_b[...] = wg_ref[...].astype(BF16)
        wl_b[...] = wl_ref[...].astype(BF16)
        wo_b[...] = wo_ref[...].astype(BF16)

        def sub(j, c):
            rs = pl.ds(pl.multiple_of(j * MOE_SUB, MOE_SUB), MOE_SUB)
            x = xbuf[rs, :]
            hg = jnp.dot(x, wg_b[...], preferred_element_type=F32) + bg_ref[...]
            hl = jnp.dot(x, wl_b[...], preferred_element_type=F32) + bl_ref[...]
            glu = jnp.minimum(hg, SWIGLU_LIMIT)
            lin = jnp.clip(hl, -SWIGLU_LIMIT, SWIGLU_LIMIT)
            act = glu * jax.nn.sigmoid(SWIGLU_ALPHA * glu) * (lin + 1.0)
            part = jnp.dot(act.astype(BF16), wo_b[...], preferred_element_type=F32)

            @pl.when(f == 0)
            def _():
                acc[rs, :] = part + bo_ref[...]

            @pl.when(f > 0)
            def _():
                acc[rs, :] = acc[rs, :] + part
            return c

        lax.fori_loop(0, n_sub, sub, 0)

    @pl.when(jnp.logical_and(f == pl.num_programs(1) - 1, n128 > 0))
    def _():
        lax.fori_loop(0, n128, lambda j, c: (out_copy(j).start(), c)[1], 0)
        lax.fori_loop(0, n128, lambda j, c: (out_copy(j).wait(), c)[1], 0)

    @pl.when(jnp.logical_and(v == pl.num_programs(0) - 1, f == pl.num_programs(1) - 1))
    def _():
        acc[0:MOE_BLOCK, :] = jnp.zeros((MOE_BLOCK, D_MODEL), F32)

        def tail_copy(j):
            r = pl.multiple_of(tail_ref[0] + j * MOE_BLOCK, MOE_BLOCK)
            return pltpu.make_async_copy(acc.at[0:MOE_BLOCK, :], y_hbm.at[pl.ds(r, MOE_BLOCK), :], sem_out)

        lax.fori_loop(0, tail_ref[1], lambda j, c: (tail_copy(j).start(), c)[1], 0)
        lax.fori_loop(0, tail_ref[1], lambda j, c: (tail_copy(j).wait(), c)[1], 0)


def _moe_experts(xs, vis_e, vis_row, vis_n, tail, w_in, b_in, w_out, b_out, p_rows):
    nv = vis_e.shape[0]
    nf = D_EXP // MOE_TF

    def tile(v, f, vn):
        return jnp.where(vn[v] > 0, f, nf - 1)

    grid_spec = pltpu.PrefetchScalarGridSpec(
        num_scalar_prefetch=4,
        grid=(nv, nf),
        in_specs=[
            pl.BlockSpec(memory_space=pl.ANY),
            pl.BlockSpec((None, D_MODEL, MOE_TF), lambda v, f, ve, vr, vn, tl: (ve[v], 0, tile(v, f, vn))),
            pl.BlockSpec((None, D_MODEL, MOE_TF), lambda v, f, ve, vr, vn, tl: (ve[v], 0, nf + tile(v, f, vn))),
            pl.BlockSpec((None, 1, MOE_TF), lambda v, f, ve, vr, vn, tl: (ve[v], 0, tile(v, f, vn))),
            pl.BlockSpec((None, 1, MOE_TF), lambda v, f, ve, vr, vn, tl: (ve[v], 0, nf + tile(v, f, vn))),
            pl.BlockSpec((None, MOE_TF, D_MODEL), lambda v, f, ve, vr, vn, tl: (ve[v], tile(v, f, vn), 0)),
            pl.BlockSpec((None, 1, D_MODEL), lambda v, f, ve, vr, vn, tl: (ve[v], 0, 0)),
        ],
        out_specs=pl.BlockSpec(memory_space=pl.ANY),
        scratch_shapes=[pltpu.VMEM((MOE_TMX, D_MODEL), BF16), pltpu.VMEM((MOE_TMX, D_MODEL), F32),
                        pltpu.VMEM((D_MODEL, MOE_TF), BF16), pltpu.VMEM((D_MODEL, MOE_TF), BF16),
                        pltpu.VMEM((MOE_TF, D_MODEL), BF16),
                        pltpu.SemaphoreType.DMA(()), pltpu.SemaphoreType.DMA(())],
    )
    return pl.pallas_call(
        _moe_kernel,
        grid_spec=grid_spec,
        out_shape=jax.ShapeDtypeStruct((p_rows, D_MODEL), F32),
        compiler_params=_cparams(("arbitrary", "arbitrary")),
        name="moe_experts",
    )(vis_e, vis_row, vis_n, tail, xs, w_in, w_in, b_in.reshape(N_EXP, 1, 2 * D_EXP),
      b_in.reshape(N_EXP, 1, 2 * D_EXP), w_out, b_out.reshape(N_EXP, 1, D_MODEL))


def _route(top_i, t):
    a = t * TOP_K
    flat_e = top_i.reshape(-1)
    onehot = (flat_e[:, None] == jnp.arange(N_EXP, dtype=jnp.int32)[None, :]).astype(jnp.int32)
    csum = jnp.cumsum(onehot, axis=0)
    rank = jnp.take_along_axis(csum, flat_e[:, None], axis=1)[:, 0] - 1
    counts = csum[-1]
    padded = (counts + MOE_BLOCK - 1) // MOE_BLOCK * MOE_BLOCK
    pends = jnp.cumsum(padded)
    pstarts = pends - padded
    dest = pstarts[flat_e] + rank

    p_rows = (a + N_EXP * (MOE_BLOCK - 1) + MOE_BLOCK - 1) // MOE_BLOCK * MOE_BLOCK
    p_alloc = p_rows + MOE_TMX
    row_tok = jnp.zeros((p_alloc,), jnp.int32).at[dest].set(jnp.arange(a, dtype=jnp.int32) // TOP_K)

    n_vis = N_EXP + (p_rows + MOE_TMX - 1) // MOE_TMX
    per_e = (padded + MOE_TMX - 1) // MOE_TMX
    vend = jnp.cumsum(per_e)
    vstart = vend - per_e
    vid = jnp.arange(n_vis, dtype=jnp.int32)
    valid = vid < vend[-1]
    e_of = jnp.clip(jnp.searchsorted(vend, vid, side="right"), 0, N_EXP - 1).astype(jnp.int32)
    e_last = jnp.clip(jnp.searchsorted(vend, vend[-1] - 1, side="right"), 0, N_EXP - 1).astype(jnp.int32)
    e_of = jnp.where(valid, e_of, e_last)
    local = vid - vstart[e_of]
    vis_row = jnp.where(valid, pstarts[e_of] + local * MOE_TMX, 0).astype(jnp.int32)
    vis_rows = jnp.where(valid, jnp.clip(padded[e_of] - local * MOE_TMX, 0, MOE_TMX), 0)
    vis_n = (vis_rows // MOE_BLOCK).astype(jnp.int32)
    tail = jnp.stack([pends[-1], (p_rows - pends[-1]) // MOE_BLOCK]).astype(jnp.int32)
    return dest.reshape(t, TOP_K), row_tok, e_of, vis_row, vis_n, tail, p_rows


def _combine_kernel(y0_ref, y1_ref, y2_ref, y3_ref, gt_ref, h_ref, g_ref, b_ref, o_ref, ob_ref):
    gt = gt_ref[...]
    f = (gt[:, 0:1] * y0_ref[...] + gt[:, 1:2] * y1_ref[...]
         + gt[:, 2:3] * y2_ref[...] + gt[:, 3:4] * y3_ref[...])
    h2 = _layer_norm(ALPHA_RES * h_ref[...] + f, g_ref[...], b_ref[...])
    o_ref[...] = h2
    ob_ref[...] = h2.astype(BF16)


def _combine(ys, gates, h, g, b, tm):
    t, d = h.shape
    row = pl.BlockSpec((tm, d), lambda i: (i, 0))
    vec = pl.BlockSpec((1, d), lambda i: (0, 0))
    return pl.pallas_call(
        _combine_kernel,
        grid=(t // tm,),
        in_specs=[row, row, row, row, pl.BlockSpec((tm, LANES), lambda i: (i, 0)), row, vec, vec],
        out_specs=[row, row],
        out_shape=[jax.ShapeDtypeStruct((t, d), F32), jax.ShapeDtypeStruct((t, d), BF16)],
        compiler_params=_cparams(("parallel",)),
        name="moe_combine",
    )(*ys, gates, h, g.reshape(1, d), b.reshape(1, d))


def kernel(x, meta_tokens, ln_in_g, ln_in_b, w_in, mu_shift, decay0, decay_up, iclr0, iclr_up,
           gate_up, k_k, k_a, r_k, lnx_g, lnx_b, conv_w, out_scale, w_out, ln1_g, ln1_b,
           w_router, b_router, w_exp_in, b_exp_in, w_exp_out, b_exp_out, ln2_g, ln2_b):
    bsz, seq, d = x.shape
    assert bsz == 1 and d == D_MODEL
    t = N_META + seq
    tm = _largest_divisor(t, 16, 512)
    tm_post = _largest_divisor(t, 16, 320)
    tb = _largest_divisor(t, CHUNK, 160)

    h0 = jnp.concatenate([meta_tokens.astype(x.dtype), x[0]], axis=0)
    h, hb = _ln_in(h0, ln_in_g, ln_in_b, tm)
    for l in range(DEPTH):
        w_in_b = w_in[l].astype(BF16)
        pr = _matmul(hb, w_in_b[:, :RW_COLS], tm, RW_COLS // 3)
        pc = _matmul(hb, w_in_b[:, RW_COLS:], tm, CONV_COLS // 3)
        r, v, a, w, iw, kd, bd, bonus, g = _prep(pr, mu_shift[l], decay0[l], decay_up[l], iclr0[l],
                                                 iclr_up[l], gate_up[l], k_k[l], k_a[l], r_k[l], tb)
        y2 = _scan(r, v, a, w, iw, kd, bd, tb)
        h1, h1b, top_i, gates = _post(y2, bonus, g, pc, h, w_out[l].astype(BF16), lnx_g[l], lnx_b[l],
                                      conv_w[l], out_scale[l], ln1_g[l], ln1_b[l],
                                      w_router[l], b_router[l], tm_post)
        pos, row_tok, vis_e, vis_row, vis_n, tail, p_rows = _route(top_i[:, :TOP_K], t)
        xs = jnp.take(h1b, row_tok, axis=0)
        yb = _moe_experts(xs, vis_e, vis_row, vis_n, tail, w_exp_in[l], b_exp_in[l], w_exp_out[l],
                          b_exp_out[l], p_rows)
        ys = [jnp.take(yb, pos[:, k], axis=0) for k in range(TOP_K)]
        h, hb = _combine(ys, gates, h1, ln2_g[l], ln2_b[l], tm)
    return h[N_META:].reshape(bsz, seq, d)
```

```python
import functools

import jax
import jax.numpy as jnp
from jax import lax
from jax.experimental import pallas as pl
from jax.experimental.pallas import tpu as pltpu

F32 = jnp.float32
BF16 = jnp.bfloat16
HI = lax.Precision.HIGHEST

D_MODEL = 2048
N_META = 16
HEAD = 64
RW = 1024
CONV = 1024
LORA = 128
RW_COLS = 3 * RW + 3 * LORA
CONV_COLS = 3 * CONV
N_EXP = 32
TOP_K = 4
D_EXP = 2048
MOE_BLOCK = 128
SWIGLU_LIMIT = 7.0
SWIGLU_ALPHA = 1.702
DEPTH = 2
ALPHA_RES = (2 * DEPTH) ** 0.25
LN_EPS = 1e-5
GN_EPS = 64e-5

LANES = 128
SUBLANES = 8
VMEM_LIMIT = 56 * 1024 * 1024

CHUNK = 16
PAIRS = RW // LANES
MOE_TMX = 1024
MOE_SUB = 256
MOE_TF = 512


def _largest_divisor(n, multiple, cap):
    best = None
    for d in range(multiple, cap + 1, multiple):
        if n % d == 0:
            best = d
    assert best is not None, (n, multiple, cap)
    return best


def _cparams(sem):
    return pltpu.CompilerParams(dimension_semantics=sem, vmem_limit_bytes=VMEM_LIMIT)


def _layer_norm(x, g, b):
    mu = jnp.mean(x, axis=-1, keepdims=True)
    xc = x - mu
    var = jnp.mean(xc * xc, axis=-1, keepdims=True)
    return xc * lax.rsqrt(var + LN_EPS) * g + b


def _head_ones():
    r = lax.broadcasted_iota(jnp.int32, (LANES, LANES), 0) >> 6
    c = lax.broadcasted_iota(jnp.int32, (LANES, LANES), 1) >> 6
    return (r == c).astype(F32)


def _head_sum(x, ones):
    return jnp.dot(x, ones, precision=HI, preferred_element_type=F32)


def _ln_in_kernel(x_ref, g_ref, b_ref, h_ref, hb_ref):
    y = _layer_norm(x_ref[...], g_ref[...], b_ref[...])
    h_ref[...] = y
    hb_ref[...] = y.astype(BF16)


def _ln_in(x, g, b, tm):
    t, d = x.shape
    row = pl.BlockSpec((tm, d), lambda i: (i, 0))
    vec = pl.BlockSpec((1, d), lambda i: (0, 0))
    return pl.pallas_call(
        _ln_in_kernel,
        grid=(t // tm,),
        in_specs=[row, vec, vec],
        out_specs=[row, row],
        out_shape=[jax.ShapeDtypeStruct((t, d), F32), jax.ShapeDtypeStruct((t, d), BF16)],
        compiler_params=_cparams(("parallel",)),
        name="ln_in",
    )(x, g.reshape(1, d), b.reshape(1, d))


def _mm_kernel(a_ref, w_ref, o_ref):
    o_ref[...] = jnp.dot(a_ref[...], w_ref[...], preferred_element_type=F32)


def _matmul(a, w, tm, tn):
    t, k = a.shape
    n = w.shape[1]
    return pl.pallas_call(
        _mm_kernel,
        grid=(n // tn, t // tm),
        in_specs=[pl.BlockSpec((tm, k), lambda j, i: (i, 0)),
                  pl.BlockSpec((k, tn), lambda j, i: (0, j))],
        out_specs=pl.BlockSpec((tm, tn), lambda j, i: (i, j)),
        out_shape=jax.ShapeDtypeStruct((t, n), F32),
        compiler_params=_cparams(("parallel", "parallel")),
        name="in_proj",
    )(a, w)


def _shift_rows(x, prev_row, next_row):
    tm = x.shape[0]
    rows = lax.broadcasted_iota(jnp.int32, x.shape, 0)
    xp = jnp.where(rows == 0, prev_row, pltpu.roll(x, 1, axis=0))
    xn = jnp.where(rows == tm - 1, next_row, pltpu.roll(x, tm - 1, axis=0))
    return xp, xn


def _halo_specs(tm, cols, nblk):
    per = tm // SUBLANES
    main = pl.BlockSpec((tm, cols), lambda i: (i, 0))
    prev = pl.BlockSpec((SUBLANES, cols), lambda i: (jnp.maximum(i * per - 1, 0), 0))
    nxt = pl.BlockSpec((SUBLANES, cols), lambda i: (jnp.minimum((i + 1) * per, nblk * per - 1), 0))
    return main, prev, nxt


def _prep_kernel(p_ref, pp_ref, pn_ref, mu_ref, d0_ref, du_ref, i0_ref, iu_ref, gu_ref,
                 kk_ref, ka_ref, rk_ref,
                 r_ref, v_ref, a_ref, w_ref, iw_ref, kd_ref, bd_ref, bonus_ref, g_ref):
    i = pl.program_id(0)
    first = i == 0
    last = i == pl.num_programs(0) - 1
    ones = _head_ones()
    tm = p_ref.shape[0]

    def mixed(c0):
        cs = slice(c0, c0 + LANES)
        x = p_ref[:, cs]
        prev_row = jnp.where(first, 0.0, pp_ref[SUBLANES - 1:SUBLANES, cs])
        next_row = jnp.where(last, 0.0, pn_ref[0:1, cs])
        xp, xn = _shift_rows(x, prev_row, next_row)
        return x + (0.5 * (xp + xn) - x) * mu_ref[:, cs]

    tanh_cw = jnp.tanh(mixed(3 * RW))
    ca = mixed(3 * RW + LORA)
    sig_cg = jax.nn.sigmoid(mixed(3 * RW + 2 * LORA))

    for s in range(RW // LANES):
        cs = slice(s * LANES, (s + 1) * LANES)
        r = mixed(s * LANES)
        k = mixed(RW + s * LANES)
        v = mixed(2 * RW + s * LANES)
        kk = k * kk_ref[:, cs]
        norm = jnp.sqrt(_head_sum(kk * kk, ones))
        kk = kk / jnp.maximum(norm, 1e-12)
        pair = pl.ds(s, tm, stride=SUBLANES)
        r_ref[pair, :] = r
        v_ref[pair, :] = v
        a_ref[pair, :] = -kk
        bonus_ref[:, cs] = _head_sum(r * k * rk_ref[:, cs], ones) * v
        g_ref[:, cs] = jnp.dot(sig_cg, gu_ref[:, cs], precision=HI, preferred_element_type=F32)
        for d in range(2):
            ds_ = slice(d * RW + s * LANES, d * RW + (s + 1) * LANES)
            z = d0_ref[:, ds_] + jnp.dot(tanh_cw, du_ref[:, ds_], precision=HI,
                                         preferred_element_type=F32)
            w_log = -(jnp.maximum(-z, 0.0) + jnp.log(1.0 + jnp.exp(-jnp.abs(z)))) - 0.5
            e1 = jnp.exp(w_log)
            w_ref[d, pair, :] = jnp.exp(-e1)
            iw_ref[d, pair, :] = jnp.exp(e1)
            a_icl = jax.nn.sigmoid(i0_ref[:, ds_] + jnp.dot(ca, iu_ref[:, ds_], precision=HI,
                                                            preferred_element_type=F32))
            kd_ref[d, pair, :] = k * (1.0 + (a_icl - 1.0) * ka_ref[:, cs])
            bd_ref[d, pair, :] = kk * a_icl


def _block_diag2(m):
    z = jnp.zeros_like(m[0])
    return jnp.concatenate([jnp.concatenate([m[0], z], axis=1),
                            jnp.concatenate([z, m[1]], axis=1)], axis=0)


def _prep(pr, mu, decay0, decay_up, iclr0, iclr_up, gate_up, k_k, k_a, r_k, tm):
    t = pr.shape[0]
    nblk = t // tm
    main, prev, nxt = _halo_specs(tm, RW_COLS, nblk)

    def full(shape):
        return pl.BlockSpec(shape, lambda i: (0,) * len(shape))

    row = pl.BlockSpec((tm, RW), lambda i: (i, 0))
    srow = pl.BlockSpec((tm * SUBLANES, LANES), lambda i: (i, 0))
    srow2 = pl.BlockSpec((2, tm * SUBLANES, LANES), lambda i: (0, i, 0))
    one = jax.ShapeDtypeStruct((t, RW), F32)
    sone = jax.ShapeDtypeStruct((t * SUBLANES, LANES), F32)
    stwo = jax.ShapeDtypeStruct((2, t * SUBLANES, LANES), F32)
    return pl.pallas_call(
        _prep_kernel,
        grid=(nblk,),
        in_specs=[main, prev, nxt, full((1, RW_COLS)), full((1, 2 * RW)), full((LORA, 2 * RW)),
                  full((1, 2 * RW)), full((LORA, 2 * RW)), full((LORA, RW)),
                  full((1, RW)), full((1, RW)), full((1, RW))],
        out_specs=[srow, srow, srow, srow2, srow2, srow2, srow2, row, row],
        out_shape=[sone, sone, sone, stwo, stwo, stwo, stwo, one, one],
        compiler_params=_cparams(("parallel",)),
        name="rwkv_prep",
    )(pr, pr, pr, mu.reshape(1, RW_COLS), decay0.reshape(1, 2 * RW), _block_diag2(decay_up),
      iclr0.reshape(1, 2 * RW), _block_diag2(iclr_up), gate_up,
      k_k.reshape(1, RW), k_a.reshape(1, RW), r_k.reshape(1, RW))


def _scan_products():
    prods = []
    for t in range(CHUNK):
        for s in range(t):
            prods.append(("ab", t, s))
    for t in range(CHUNK):
        for s in range(t):
            prods.append(("ak", t, s))
    for t in range(CHUNK):
        for s in range(t + 1):
            prods.append(("rb", t, s))
    for t in range(CHUNK):
        for s in range(t + 1):
            prods.append(("rk", t, s))
    return prods


_PRODS = _scan_products()
_PROD_ROW = {p: SUBLANES * n for n, p in enumerate(_PRODS)}
_N_PROD_ROWS = SUBLANES * len(_PRODS)
_PROD_TILE = 512
_NT = (((1,), (1,)), ((), ()))
_TN = (((0,), (0,)), ((), ()))


def _split_bf16(x):
    hi = x.astype(BF16)
    return hi, (x - hi.astype(F32)).astype(BF16)


def _split_trunc(x):
    bits = lax.bitcast_convert_type(x, jnp.uint32) & jnp.uint32(0xFFFF0000)
    hi = lax.bitcast_convert_type(bits, F32)
    return hi, x - hi


def _scan_chunk(c, n_chunks, rev, r_ref, v_ref, a_ref, w_ref, iw_ref, k_ref, b_ref, y_ref,
                s_ref, sh_ref, sl_ref, at_ref, rt_ref, bt_ref, kt_ref, ut_ref, sa_ref, sr_ref,
                gc_ref, lhs_ref, m_ref, ones2, diag):
    c16 = CHUNK * SUBLANES
    cn = (n_chunks - 1 - c) if rev else c
    base = pl.multiple_of(cn * c16, c16)

    def loc(tau):
        return ((CHUNK - 1 - tau) if rev else tau) * SUBLANES

    def rows(ref, tau):
        return ref[pl.ds(pl.multiple_of(base + loc(tau), SUBLANES), SUBLANES), :]

    def local(ref, tau):
        return ref[loc(tau):loc(tau) + SUBLANES, :]

    def pair_rows(ref, p):
        return ref[pl.ds(p, CHUNK, stride=SUBLANES), :]

    g = None
    ig = None
    for tau in range(CHUNK):
        sl = slice(loc(tau), loc(tau) + SUBLANES)
        a = rows(a_ref, tau)
        at_ref[sl, :] = a if g is None else a * g
        w = rows(w_ref, tau)
        iw = rows(iw_ref, tau)
        g = w if g is None else g * w
        ig = iw if ig is None else ig * iw
        rt_ref[sl, :] = rows(r_ref, tau) * g
        bt_ref[sl, :] = rows(b_ref, tau) * ig
        kt_ref[sl, :] = rows(k_ref, tau) * ig
    gc_ref[...] = g

    for p in range(PAIRS):
        ar = jnp.concatenate([pair_rows(at_ref, p), pair_rows(rt_ref, p)], axis=0)
        ar_hi, ar_lo = _split_bf16(ar)
        r1 = lax.dot_general(jnp.concatenate([ar_hi, ar_lo], axis=0), sh_ref[p], _NT,
                             preferred_element_type=F32)
        r2 = lax.dot_general(ar_hi, sl_ref[p], _NT, preferred_element_type=F32)
        res = r1[:2 * CHUNK] + r1[2 * CHUNK:] + r2
        sa_ref[pl.ds(p, CHUNK, stride=SUBLANES), :] = res[:CHUNK]
        sr_ref[pl.ds(p, CHUNK, stride=SUBLANES), :] = res[CHUNK:]

    srcs = {"a": at_ref, "r": rt_ref, "b": bt_ref, "k": kt_ref}

    def prod(n):
        kind, t, s = _PRODS[n]
        return local(srcs[kind[0]], t) * local(srcs[kind[1]], s)

    for m in range(len(_PRODS) // 2):
        hi, lo = _split_bf16(jnp.concatenate([prod(2 * m), prod(2 * m + 1)], axis=0))
        lhs_ref[16 * m:16 * (m + 1), 0:LANES] = hi
        lhs_ref[16 * m:16 * (m + 1), LANES:2 * LANES] = lo
    for q in range(_N_PROD_ROWS // _PROD_TILE):
        sl = slice(q * _PROD_TILE, (q + 1) * _PROD_TILE)
        m_ref[sl, :] = jnp.dot(lhs_ref[sl, :], ones2, preferred_element_type=F32)

    def coef(kind, t, s):
        r0 = _PROD_ROW[(kind, t, s)]
        return m_ref[r0:r0 + SUBLANES, :]

    vs = [rows(v_ref, tau) for tau in range(CHUNK)]
    us = []
    for t in range(CHUNK):
        acc = local(sa_ref, t)
        for s in range(t):
            acc = acc + coef("ak", t, s) * vs[s]
        for s in range(t):
            acc = acc + coef("ab", t, s) * us[s]
        us.append(acc)
        ut_ref[loc(t):loc(t) + SUBLANES, :] = acc

    for t in range(CHUNK):
        acc = local(sr_ref, t)
        for s in range(t + 1):
            acc = acc + coef("rk", t, s) * vs[s] + coef("rb", t, s) * us[s]
        y_ref[pl.ds(pl.multiple_of(base + loc(t), SUBLANES), SUBLANES), :] = acc

    for p in range(PAIRS):
        gp = gc_ref[p:p + 1, :]
        bk = jnp.concatenate([pair_rows(bt_ref, p), pair_rows(kt_ref, p)], axis=0) * gp
        uv = jnp.concatenate([pair_rows(ut_ref, p),
                              v_ref[pl.ds(base + p, CHUNK, stride=SUBLANES), :]], axis=0)
        bk_hi, bk_lo = _split_trunc(bk)
        uv_hi, uv_lo = _split_trunc(uv)
        upd = lax.dot_general(jnp.concatenate([uv_hi, uv_lo, uv_hi], axis=0),
                              jnp.concatenate([bk_hi, bk_hi, bk_lo], axis=0), _TN,
                              preferred_element_type=F32)
        s_new = s_ref[p] * gp + jnp.where(diag, upd, 0.0)
        s_ref[p] = s_new
        s_hi, s_lo = _split_bf16(s_new)
        sh_ref[p] = s_hi
        sl_ref[p] = s_lo


_SCAN_SCRATCH_PER_DIR = 13


def _scan_kernel(*refs, n_chunks):
    ins, outs, scr = refs[:14], refs[14:16], refs[16:]
    rf, vf, af, rb, vb, ab, wf, iwf, kf, bf, wb, iwb, kb, bb = ins
    fwd = (rf, vf, af, wf, iwf, kf, bf, outs[0]) + tuple(scr[:_SCAN_SCRATCH_PER_DIR])
    bwd = (rb, vb, ab, wb, iwb, kb, bb, outs[1]) + tuple(scr[_SCAN_SCRATCH_PER_DIR:])

    @pl.when(pl.program_id(0) == 0)
    def _():
        for st in (fwd, bwd):
            for ref in st[8:11]:
                ref[...] = jnp.zeros_like(ref)

    rr = lax.broadcasted_iota(jnp.int32, (2 * LANES, LANES), 0)
    cc = lax.broadcasted_iota(jnp.int32, (2 * LANES, LANES), 1)
    ones2 = (((rr >> 6) & 1) == (cc >> 6)).astype(BF16)
    ri = lax.broadcasted_iota(jnp.int32, (LANES, LANES), 0) >> 6
    ci = lax.broadcasted_iota(jnp.int32, (LANES, LANES), 1) >> 6
    diag = ri == ci

    def body(c, carry):
        _scan_chunk(c, n_chunks, False, *fwd, ones2, diag)
        _scan_chunk(c, n_chunks, True, *bwd, ones2, diag)
        return carry

    lax.fori_loop(0, n_chunks, body, 0)


def _scan(r, v, a, w, iw, kd, bd, tb):
    t = r.shape[0] // SUBLANES
    nb = t // tb
    rows = tb * SUBLANES
    sf = pl.BlockSpec((rows, LANES), lambda i: (i, 0))
    sb = pl.BlockSpec((rows, LANES), lambda i: (nb - 1 - i, 0))
    df = pl.BlockSpec((None, rows, LANES), lambda i: (0, i, 0))
    db = pl.BlockSpec((None, rows, LANES), lambda i: (1, nb - 1 - i, 0))
    c16 = CHUNK * SUBLANES
    per_dir = ([pltpu.VMEM((PAIRS, LANES, LANES), F32),
                pltpu.VMEM((PAIRS, LANES, LANES), BF16), pltpu.VMEM((PAIRS, LANES, LANES), BF16)]
               + [pltpu.VMEM((c16, LANES), F32) for _ in range(7)]
               + [pltpu.VMEM((SUBLANES, LANES), F32),
                  pltpu.VMEM((_N_PROD_ROWS, 2 * LANES), BF16),
                  pltpu.VMEM((_N_PROD_ROWS, LANES), F32)])
    assert len(per_dir) == _SCAN_SCRATCH_PER_DIR
    out = jax.ShapeDtypeStruct((t * SUBLANES, LANES), F32)
    return pl.pallas_call(
        functools.partial(_scan_kernel, n_chunks=tb // CHUNK),
        grid=(nb,),
        in_specs=[sf, sf, sf, sb, sb, sb, df, df, df, df, db, db, db, db],
        out_specs=[sf, sb],
        out_shape=[out, out],
        scratch_shapes=per_dir + per_dir,
        compiler_params=_cparams(("arbitrary",)),
        name="rwkv_scan",
    )(r, v, a, r, v, a, w, iw, kd, bd, w, iw, kd, bd)


def _top4(logits):
    lane = lax.broadcasted_iota(jnp.int32, logits.shape, 1).astype(F32)
    vals, idxs = [], []
    l = logits
    for _ in range(TOP_K):
        m = jnp.max(l, axis=-1, keepdims=True)
        idx = jnp.min(jnp.where(l == m, lane, float(LANES)), axis=-1, keepdims=True)
        vals.append(m)
        idxs.append(idx)
        l = jnp.where(lane == idx, -jnp.inf, l)
    es = [jnp.exp(vk - vals[0]) for vk in vals]
    tot = es[0] + es[1] + es[2] + es[3]
    top_i = jnp.zeros(logits.shape, F32)
    gates = jnp.zeros(logits.shape, F32)
    for k in range(TOP_K):
        top_i = jnp.where(lane == float(k), idxs[k], top_i)
        gates = jnp.where(lane == float(k), es[k] / tot, gates)
    return top_i.astype(jnp.int32), gates


def _post_kernel(yf_ref, yb_ref, bonus_ref, g_ref, c_ref, cp_ref, cn_ref, h_ref, wo_ref,
                 lg_ref, lb_ref, cw_ref, os_ref, n1g_ref, n1b_ref, wr_ref, br_ref,
                 h1_ref, ti_ref, gt_ref, ocat_ref):
    i = pl.program_id(0)
    first = i == 0
    last = i == pl.num_programs(0) - 1
    ones = _head_ones()
    inv_n = 1.0 / HEAD
    tm = h_ref.shape[0]

    for s in range(RW // LANES):
        cs = slice(s * LANES, (s + 1) * LANES)
        pair = pl.ds(s, tm, stride=SUBLANES)
        y = yf_ref[pair, :] + yb_ref[pair, :]
        mu = _head_sum(y, ones) * inv_n
        yc = y - mu
        var = _head_sum(yc * yc, ones) * inv_n
        yn = yc * lax.rsqrt(var + GN_EPS) * lg_ref[:, cs] + lb_ref[:, cs] + bonus_ref[:, cs]
        ocat_ref[:, cs] = (yn * g_ref[:, cs] * os_ref[:, cs]).astype(BF16)

    for s in range(CONV // LANES):
        cs = slice(s * LANES, (s + 1) * LANES)
        gs = slice(CONV + s * LANES, CONV + (s + 1) * LANES)
        hs = slice(2 * CONV + s * LANES, 2 * CONV + (s + 1) * LANES)
        u = c_ref[:, gs] * c_ref[:, hs]
        u_prev = jnp.where(first, 0.0, cp_ref[SUBLANES - 1:SUBLANES, gs] * cp_ref[SUBLANES - 1:SUBLANES, hs])
        u_next = jnp.where(last, 0.0, cn_ref[0:1, gs] * cn_ref[0:1, hs])
        up, un = _shift_rows(u, u_prev, u_next)
        conv = cw_ref[0:1, cs] * up + cw_ref[1:2, cs] * u + cw_ref[2:3, cs] * un
        oc = slice(RW + s * LANES, RW + (s + 1) * LANES)
        ocat_ref[:, oc] = (c_ref[:, cs] * conv * os_ref[:, oc]).astype(BF16)

    m = jnp.dot(ocat_ref[...], wo_ref[...], preferred_element_type=F32)
    h1 = _layer_norm(ALPHA_RES * h_ref[...] + m, n1g_ref[...], n1b_ref[...])
    h1_ref[...] = h1
    logits = jnp.dot(h1, wr_ref[...], precision=HI, preferred_element_type=F32) + br_ref[...]
    ti, gt = _top4(logits)
    ti_ref[...] = ti
    gt_ref[...] = gt


def _post(yf, yb, bonus, g, pc, h, w_out_b, lnx_g, lnx_b, conv_w, out_scale, ln1_g, ln1_b,
          w_router, b_router, tm):
    t = h.shape[0]
    nblk = t // tm
    main, prev, nxt = _halo_specs(tm, CONV_COLS, nblk)

    def full(shape):
        return pl.BlockSpec(shape, lambda i: (0,) * len(shape))

    row = lambda c: pl.BlockSpec((tm, c), lambda i: (i, 0))
    srow = pl.BlockSpec((tm * SUBLANES, LANES), lambda i: (i, 0))
    wr = jnp.zeros((D_MODEL, LANES), F32).at[:, :N_EXP].set(w_router)
    br = jnp.full((1, LANES), -1e30, F32).at[0, :N_EXP].set(b_router)
    return pl.pallas_call(
        _post_kernel,
        grid=(nblk,),
        in_specs=[srow, srow, row(RW), row(RW),
                  main, prev, nxt, row(D_MODEL), full((D_MODEL, D_MODEL)),
                  full((1, RW)), full((1, RW)), full((3, CONV)), full((1, D_MODEL)),
                  full((1, D_MODEL)), full((1, D_MODEL)), full((D_MODEL, LANES)), full((1, LANES))],
        out_specs=[row(D_MODEL), row(LANES), row(LANES)],
        out_shape=[jax.ShapeDtypeStruct((t, D_MODEL), F32),
                   jax.ShapeDtypeStruct((t, LANES), jnp.int32), jax.ShapeDtypeStruct((t, LANES), F32)],
        scratch_shapes=[pltpu.VMEM((tm, D_MODEL), BF16)],
        compiler_params=_cparams(("parallel",)),
        name="mixer_out",
    )(yf, yb, bonus, g, pc, pc, pc, h, w_out_b, lnx_g.reshape(1, RW), lnx_b.reshape(1, RW), conv_w,
      out_scale.reshape(1, D_MODEL), ln1_g.reshape(1, D_MODEL), ln1_b.reshape(1, D_MODEL), wr, br)


def _moe_kernel(ve_ref, vr_ref, vn_ref, tail_ref, tok_ref, x_hbm, wg_ref, wl_ref, bg_ref, bl_ref,
                wo_ref, bo_ref, y_hbm, xbuf, acc, wg_b, wl_b, wo_b, sem_in, sem_out):
    v = pl.program_id(0)
    f = pl.program_id(1)
    n128 = vn_ref[v]
    n_sub = (n128 + 1) >> 1
    row0 = pl.multiple_of(vr_ref[v], MOE_BLOCK)

    def in_copy(j):
        return pltpu.make_async_copy(x_hbm.at[pl.ds(tok_ref[0, j], 1), :],
                                     xbuf.at[pl.ds(j, 1), :], sem_in)

    def for_rows(n_groups, fn):
        def group(gi, c):
            for u in range(SUBLANES):
                fn(gi * SUBLANES + u)
            return c
        lax.fori_loop(0, n_groups, group, 0)

    def out_copy(j):
        r = pl.multiple_of(j * MOE_BLOCK, MOE_BLOCK)
        return pltpu.make_async_copy(acc.at[pl.ds(r, MOE_BLOCK), :],
                                     y_hbm.at[pl.ds(row0 + r, MOE_BLOCK), :], sem_out)

    @pl.when(jnp.logical_and(f == 0, n128 > 0))
    def _():
        n_groups = n_sub * (MOE_SUB // SUBLANES)
        for_rows(n_groups, lambda j: in_copy(j).start())
        for_rows(n_groups, lambda j: in_copy(j).wait())

    @pl.when(n128 > 0)
    def _():
        wg_b[...] = wg_ref[...].astype(BF16)
        wl_b[...] = wl_ref[...].astype(BF16)
        wo_b[...] = wo_ref[...].astype(BF16)

        def sub(j, c):
            rs = pl.ds(pl.multiple_of(j * MOE_SUB, MOE_SUB), MOE_SUB)
            x = xbuf[rs, :].astype(BF16)
            hg = jnp.dot(x, wg_b[...], preferred_element_type=F32) + bg_ref[...]
            hl = jnp.dot(x, wl_b[...], preferred_element_type=F32) + bl_ref[...]
            glu = jnp.minimum(hg, SWIGLU_LIMIT)
            lin = jnp.clip(hl, -SWIGLU_LIMIT, SWIGLU_LIMIT)
            act = glu * jax.nn.sigmoid(SWIGLU_ALPHA * glu) * (lin + 1.0)
            part = jnp.dot(act.astype(BF16), wo_b[...], preferred_element_type=F32)

            @pl.when(f == 0)
            def _():
                acc[rs, :] = part + bo_ref[...]

            @pl.when(f > 0)
            def _():
                acc[rs, :] = acc[rs, :] + part
            return c

        lax.fori_loop(0, n_sub, sub, 0)

    @pl.when(jnp.logical_and(f == pl.num_programs(1) - 1, n128 > 0))
    def _():
        lax.fori_loop(0, n128, lambda j, c: (out_copy(j).start(), c)[1], 0)
        lax.fori_loop(0, n128, lambda j, c: (out_copy(j).wait(), c)[1], 0)

    @pl.when(jnp.logical_and(v == pl.num_programs(0) - 1, f == pl.num_programs(1) - 1))
    def _():
        acc[0:MOE_BLOCK, :] = jnp.zeros((MOE_BLOCK, D_MODEL), F32)

        def tail_copy(j):
            r = pl.multiple_of(tail_ref[0] + j * MOE_BLOCK, MOE_BLOCK)
            return pltpu.make_async_copy(acc.at[0:MOE_BLOCK, :], y_hbm.at[pl.ds(r, MOE_BLOCK), :], sem_out)

        lax.fori_loop(0, tail_ref[1], lambda j, c: (tail_copy(j).start(), c)[1], 0)
        lax.fori_loop(0, tail_ref[1], lambda j, c: (tail_copy(j).wait(), c)[1], 0)


def _moe_experts(h, vis_tok, vis_e, vis_row, vis_n, tail, layer, w_in, b_in, w_out, b_out, p_rows):
    nv = vis_e.shape[0]
    nf = D_EXP // MOE_TF

    def tile(v, f, vn):
        return jnp.where(vn[v] > 0, f, nf - 1)

    grid_spec = pltpu.PrefetchScalarGridSpec(
        num_scalar_prefetch=4,
        grid=(nv, nf),
        in_specs=[
            pl.BlockSpec((None, 1, MOE_TMX), lambda v, f, ve, vr, vn, tl: (v, 0, 0),
                         memory_space=pltpu.SMEM),
            pl.BlockSpec(memory_space=pl.ANY),
            pl.BlockSpec((None, None, D_MODEL, MOE_TF), lambda v, f, ve, vr, vn, tl: (layer, ve[v], 0, tile(v, f, vn))),
            pl.BlockSpec((None, None, D_MODEL, MOE_TF), lambda v, f, ve, vr, vn, tl: (layer, ve[v], 0, nf + tile(v, f, vn))),
            pl.BlockSpec((None, None, 1, MOE_TF), lambda v, f, ve, vr, vn, tl: (layer, ve[v], 0, tile(v, f, vn))),
            pl.BlockSpec((None, None, 1, MOE_TF), lambda v, f, ve, vr, vn, tl: (layer, ve[v], 0, nf + tile(v, f, vn))),
            pl.BlockSpec((None, None, MOE_TF, D_MODEL), lambda v, f, ve, vr, vn, tl: (layer, ve[v], tile(v, f, vn), 0)),
            pl.BlockSpec((None, None, 1, D_MODEL), lambda v, f, ve, vr, vn, tl: (layer, ve[v], 0, 0)),
        ],
        out_specs=pl.BlockSpec(memory_space=pl.ANY),
        scratch_shapes=[pltpu.VMEM((MOE_TMX, D_MODEL), F32), pltpu.VMEM((MOE_TMX, D_MODEL), F32),
                        pltpu.VMEM((D_MODEL, MOE_TF), BF16), pltpu.VMEM((D_MODEL, MOE_TF), BF16),
                        pltpu.VMEM((MOE_TF, D_MODEL), BF16),
                        pltpu.SemaphoreType.DMA(()), pltpu.SemaphoreType.DMA(())],
    )
    return pl.pallas_call(
        _moe_kernel,
        grid_spec=grid_spec,
        out_shape=jax.ShapeDtypeStruct((p_rows, D_MODEL), F32),
        compiler_params=_cparams(("arbitrary", "arbitrary")),
        name="moe_experts",
    )(vis_e, vis_row, vis_n, tail, vis_tok, h, w_in, w_in, b_in.reshape(DEPTH, N_EXP, 1, 2 * D_EXP),
      b_in.reshape(DEPTH, N_EXP, 1, 2 * D_EXP), w_out, b_out.reshape(DEPTH, N_EXP, 1, D_MODEL))


def _route(top_i, t):
    a = t * TOP_K
    flat_e = top_i.reshape(-1)
    onehot = (flat_e[:, None] == jnp.arange(N_EXP, dtype=jnp.int32)[None, :]).astype(jnp.int32)
    csum = jnp.cumsum(onehot, axis=0)
    rank = jnp.take_along_axis(csum, flat_e[:, None], axis=1)[:, 0] - 1
    counts = csum[-1]
    padded = (counts + MOE_BLOCK - 1) // MOE_BLOCK * MOE_BLOCK
    pends = jnp.cumsum(padded)
    pstarts = pends - padded
    dest = pstarts[flat_e] + rank

    p_rows = (a + N_EXP * (MOE_BLOCK - 1) + MOE_BLOCK - 1) // MOE_BLOCK * MOE_BLOCK
    p_alloc = p_rows + MOE_TMX
    row_tok = jnp.zeros((p_alloc,), jnp.int32).at[dest].set(jnp.arange(a, dtype=jnp.int32) // TOP_K)

    n_vis = N_EXP + (p_rows + MOE_TMX - 1) // MOE_TMX
    per_e = (padded + MOE_TMX - 1) // MOE_TMX
    vend = jnp.cumsum(per_e)
    vstart = vend - per_e
    vid = jnp.arange(n_vis, dtype=jnp.int32)
    valid = vid < vend[-1]
    e_of = jnp.clip(jnp.searchsorted(vend, vid, side="right"), 0, N_EXP - 1).astype(jnp.int32)
    e_last = jnp.clip(jnp.searchsorted(vend, vend[-1] - 1, side="right"), 0, N_EXP - 1).astype(jnp.int32)
    e_of = jnp.where(valid, e_of, e_last)
    local = vid - vstart[e_of]
    vis_row = jnp.where(valid, pstarts[e_of] + local * MOE_TMX, 0).astype(jnp.int32)
    vis_rows = jnp.where(valid, jnp.clip(padded[e_of] - local * MOE_TMX, 0, MOE_TMX), 0)
    vis_n = (vis_rows // MOE_BLOCK).astype(jnp.int32)
    tail = jnp.stack([pends[-1], (p_rows - pends[-1]) // MOE_BLOCK]).astype(jnp.int32)
    vis_tok = row_tok[vis_row[:, None] + jnp.arange(MOE_TMX, dtype=jnp.int32)[None, :]]
    return dest, vis_tok.reshape(n_vis, 1, MOE_TMX), e_of, vis_row, vis_n, tail, p_rows


def _combine_kernel(pos_ref, y_hbm, gt_ref, h_ref, g_ref, b_ref, o_ref, ob_ref, buf, sem):
    tm = h_ref.shape[0]

    def copy(t, k):
        return pltpu.make_async_copy(y_hbm.at[pl.ds(pos_ref[0, t * TOP_K + k], 1), :],
                                     buf.at[k, pl.ds(t, 1), :], sem)

    def for_tokens(fn):
        def group(gi, c):
            for u in range(SUBLANES):
                for k in range(TOP_K):
                    fn(gi * SUBLANES + u, k)
            return c
        lax.fori_loop(0, tm // SUBLANES, group, 0)

    for_tokens(lambda t, k: copy(t, k).start())
    for_tokens(lambda t, k: copy(t, k).wait())
    gt = gt_ref[...]
    f = (gt[:, 0:1] * buf[0] + gt[:, 1:2] * buf[1] + gt[:, 2:3] * buf[2] + gt[:, 3:4] * buf[3])
    h2 = _layer_norm(ALPHA_RES * h_ref[...] + f, g_ref[...], b_ref[...])
    o_ref[...] = h2
    ob_ref[...] = h2.astype(BF16)


def _combine(yb, pos, gates, h, g, b, tm):
    t, d = h.shape
    nt = t // tm
    row = pl.BlockSpec((tm, d), lambda i: (i, 0))
    vec = pl.BlockSpec((1, d), lambda i: (0, 0))
    return pl.pallas_call(
        _combine_kernel,
        grid=(nt,),
        in_specs=[pl.BlockSpec((None, 1, tm * TOP_K), lambda i: (i, 0, 0), memory_space=pltpu.SMEM),
                  pl.BlockSpec(memory_space=pl.ANY),
                  pl.BlockSpec((tm, LANES), lambda i: (i, 0)), row, vec, vec],
        out_specs=[row, row],
        out_shape=[jax.ShapeDtypeStruct((t, d), F32), jax.ShapeDtypeStruct((t, d), BF16)],
        scratch_shapes=[pltpu.VMEM((TOP_K, tm, d), F32), pltpu.SemaphoreType.DMA(())],
        compiler_params=_cparams(("arbitrary",)),
        name="moe_combine",
    )(pos.reshape(nt, 1, tm * TOP_K), yb, gates, h, g.reshape(1, d), b.reshape(1, d))


def kernel(x, meta_tokens, ln_in_g, ln_in_b, w_in, mu_shift, decay0, decay_up, iclr0, iclr_up,
           gate_up, k_k, k_a, r_k, lnx_g, lnx_b, conv_w, out_scale, w_out, ln1_g, ln1_b,
           w_router, b_router, w_exp_in, b_exp_in, w_exp_out, b_exp_out, ln2_g, ln2_b):
    bsz, seq, d = x.shape
    assert bsz == 1 and d == D_MODEL
    t = N_META + seq
    tm = _largest_divisor(t, 16, 512)
    tm_post = _largest_divisor(t, 16, 320)
    tb = _largest_divisor(t, CHUNK, 160)

    h0 = jnp.concatenate([meta_tokens.astype(x.dtype), x[0]], axis=0)
    h, hb = _ln_in(h0, ln_in_g, ln_in_b, tm)
    for l in range(DEPTH):
        w_in_b = w_in[l].astype(BF16)
        pr = _matmul(hb, w_in_b[:, :RW_COLS], tm, RW_COLS // 3)
        pc = _matmul(hb, w_in_b[:, RW_COLS:], tm, CONV_COLS // 3)
        r, v, a, w, iw, kd, bd, bonus, g = _prep(pr, mu_shift[l], decay0[l], decay_up[l], iclr0[l],
                                                 iclr_up[l], gate_up[l], k_k[l], k_a[l], r_k[l], tb)
        yf, yb_ = _scan(r, v, a, w, iw, kd, bd, tb)
        h1, top_i, gates = _post(yf, yb_, bonus, g, pc, h, w_out[l].astype(BF16), lnx_g[l], lnx_b[l],
                                      conv_w[l], out_scale[l], ln1_g[l], ln1_b[l],
                                      w_router[l], b_router[l], tm_post)
        pos, vis_tok, vis_e, vis_row, vis_n, tail, p_rows = _route(top_i[:, :TOP_K], t)
        yb = _moe_experts(h1, vis_tok, vis_e, vis_row, vis_n, tail, l, w_exp_in, b_exp_in, w_exp_out,
                          b_exp_out, p_rows)
        h, hb = _combine(yb, pos, gates, h1, ln2_g[l], ln2_b[l], tm_post)
    return h[N_META:].reshape(bsz, seq, d)
```

```python
import functools

import jax
import jax.numpy as jnp
from jax import lax
from jax.experimental import pallas as pl
from jax.experimental.pallas import tpu as pltpu

F32 = jnp.float32
BF16 = jnp.bfloat16
HI = lax.Precision.HIGHEST

D_MODEL = 2048
N_META = 16
HEAD = 64
RW = 1024
CONV = 1024
LORA = 128
RW_COLS = 3 * RW + 3 * LORA
CONV_COLS = 3 * CONV
N_EXP = 32
TOP_K = 4
D_EXP = 2048
MOE_BLOCK = 128
SWIGLU_LIMIT = 7.0
SWIGLU_ALPHA = 1.702
DEPTH = 2
ALPHA_RES = (2 * DEPTH) ** 0.25
LN_EPS = 1e-5
GN_EPS = 64e-5

LANES = 128
SUBLANES = 8
VMEM_LIMIT = 56 * 1024 * 1024

CHUNK = 16
PAIRS = RW // LANES
MOE_TMX = 1536
MOE_SUB = 256
MOE_TF = 256


def _largest_divisor(n, multiple, cap):
    best = None
    for d in range(multiple, cap + 1, multiple):
        if n % d == 0:
            best = d
    assert best is not None, (n, multiple, cap)
    return best


def _cparams(sem):
    return pltpu.CompilerParams(dimension_semantics=sem, vmem_limit_bytes=VMEM_LIMIT)


def _layer_norm(x, g, b):
    mu = jnp.mean(x, axis=-1, keepdims=True)
    xc = x - mu
    var = jnp.mean(xc * xc, axis=-1, keepdims=True)
    return xc * lax.rsqrt(var + LN_EPS) * g + b


def _head_ones2():
    r = (lax.broadcasted_iota(jnp.int32, (2 * LANES, LANES), 0) >> 6) & 1
    c = lax.broadcasted_iota(jnp.int32, (2 * LANES, LANES), 1) >> 6
    return (r == c).astype(BF16)


def _split_bf16(x):
    hi = x.astype(BF16)
    return hi, (x - hi.astype(F32)).astype(BF16)


def _head_sum(x, ones2):
    hi, lo = _split_bf16(x)
    return jnp.dot(jnp.concatenate([hi, lo], axis=1), ones2, preferred_element_type=F32)


def _ln_in_kernel(x_ref, g_ref, b_ref, h_ref, hb_ref):
    y = _layer_norm(x_ref[...], g_ref[...], b_ref[...])
    h_ref[...] = y
    hb_ref[...] = y.astype(BF16)


def _ln_in(x, g, b, tm):
    t, d = x.shape
    row = pl.BlockSpec((tm, d), lambda i: (i, 0))
    vec = pl.BlockSpec((1, d), lambda i: (0, 0))
    return pl.pallas_call(
        _ln_in_kernel,
        grid=(t // tm,),
        in_specs=[row, vec, vec],
        out_specs=[row, row],
        out_shape=[jax.ShapeDtypeStruct((t, d), F32), jax.ShapeDtypeStruct((t, d), BF16)],
        compiler_params=_cparams(("parallel",)),
        name="ln_in",
    )(x, g.reshape(1, d), b.reshape(1, d))


def _mm_kernel(a_ref, w_ref, o_ref):
    o_ref[...] = jnp.dot(a_ref[...], w_ref[...], preferred_element_type=F32)


def _matmul(a, w, tm, tn):
    t, k = a.shape
    n = w.shape[1]
    return pl.pallas_call(
        _mm_kernel,
        grid=(n // tn, t // tm),
        in_specs=[pl.BlockSpec((tm, k), lambda j, i: (i, 0)),
                  pl.BlockSpec((k, tn), lambda j, i: (0, j))],
        out_specs=pl.BlockSpec((tm, tn), lambda j, i: (i, j)),
        out_shape=jax.ShapeDtypeStruct((t, n), F32),
        compiler_params=_cparams(("parallel", "parallel")),
        name="in_proj",
    )(a, w)


def _shift_rows(x, prev_row, next_row):
    tm = x.shape[0]
    rows = lax.broadcasted_iota(jnp.int32, x.shape, 0)
    xp = jnp.where(rows == 0, prev_row, pltpu.roll(x, 1, axis=0))
    xn = jnp.where(rows == tm - 1, next_row, pltpu.roll(x, tm - 1, axis=0))
    return xp, xn


def _halo_specs(tm, cols, nblk):
    per = tm // SUBLANES
    main = pl.BlockSpec((tm, cols), lambda i: (i, 0))
    prev = pl.BlockSpec((SUBLANES, cols), lambda i: (jnp.maximum(i * per - 1, 0), 0))
    nxt = pl.BlockSpec((SUBLANES, cols), lambda i: (jnp.minimum((i + 1) * per, nblk * per - 1), 0))
    return main, prev, nxt


def _prep_kernel(p_ref, pp_ref, pn_ref, mu_ref, d0_ref, du_ref, i0_ref, iu_ref, gu_ref,
                 kk_ref, ka_ref, rk_ref,
                 r_ref, v_ref, a_ref, w_ref, iw_ref, kd_ref, bd_ref, bonus_ref, g_ref):
    i = pl.program_id(0)
    first = i == 0
    last = i == pl.num_programs(0) - 1
    ones = _head_ones2()
    tm = p_ref.shape[0]

    def mixed(c0):
        cs = slice(c0, c0 + LANES)
        x = p_ref[:, cs]
        prev_row = jnp.where(first, 0.0, pp_ref[SUBLANES - 1:SUBLANES, cs])
        next_row = jnp.where(last, 0.0, pn_ref[0:1, cs])
        xp, xn = _shift_rows(x, prev_row, next_row)
        return x + (0.5 * (xp + xn) - x) * mu_ref[:, cs]

    tanh_cw = jnp.tanh(mixed(3 * RW)).astype(BF16)
    ca = mixed(3 * RW + LORA).astype(BF16)
    sig_cg = jax.nn.sigmoid(mixed(3 * RW + 2 * LORA)).astype(BF16)

    for s in range(RW // LANES):
        cs = slice(s * LANES, (s + 1) * LANES)
        r = mixed(s * LANES)
        k = mixed(RW + s * LANES)
        v = mixed(2 * RW + s * LANES)
        kk = k * kk_ref[:, cs]
        norm = jnp.sqrt(_head_sum(kk * kk, ones))
        kk = kk / jnp.maximum(norm, 1e-12)
        pair = pl.ds(s, tm, stride=SUBLANES)
        r_ref[pair, :] = r
        v_ref[pair, :] = v
        a_ref[pair, :] = -kk
        bonus_ref[:, cs] = _head_sum(r * k * rk_ref[:, cs], ones) * v
        g_ref[:, cs] = jnp.dot(sig_cg, gu_ref[:, cs], preferred_element_type=F32)
        for d in range(2):
            ds_ = slice(d * RW + s * LANES, d * RW + (s + 1) * LANES)
            z = d0_ref[:, ds_] + jnp.dot(tanh_cw, du_ref[:, ds_], preferred_element_type=F32)
            w_log = -(jnp.maximum(-z, 0.0) + jnp.log(1.0 + jnp.exp(-jnp.abs(z)))) - 0.5
            e1 = jnp.exp(w_log)
            w_ref[d, pair, :] = jnp.exp(-e1)
            iw_ref[d, pair, :] = jnp.exp(e1)
            a_icl = jax.nn.sigmoid(i0_ref[:, ds_] + jnp.dot(ca, iu_ref[:, ds_],
                                                            preferred_element_type=F32))
            kd_ref[d, pair, :] = k * (1.0 + (a_icl - 1.0) * ka_ref[:, cs])
            bd_ref[d, pair, :] = kk * a_icl


def _block_diag2(m):
    z = jnp.zeros_like(m[0])
    return jnp.concatenate([jnp.concatenate([m[0], z], axis=1),
                            jnp.concatenate([z, m[1]], axis=1)], axis=0)


def _prep(pr, mu, decay0, decay_up, iclr0, iclr_up, gate_up, k_k, k_a, r_k, tm):
    t = pr.shape[0]
    nblk = t // tm
    main, prev, nxt = _halo_specs(tm, RW_COLS, nblk)

    def full(shape):
        return pl.BlockSpec(shape, lambda i: (0,) * len(shape))

    row = pl.BlockSpec((tm, RW), lambda i: (i, 0))
    srow = pl.BlockSpec((tm * SUBLANES, LANES), lambda i: (i, 0))
    srow2 = pl.BlockSpec((2, tm * SUBLANES, LANES), lambda i: (0, i, 0))
    one = jax.ShapeDtypeStruct((t, RW), F32)
    sone = jax.ShapeDtypeStruct((t * SUBLANES, LANES), F32)
    stwo = jax.ShapeDtypeStruct((2, t * SUBLANES, LANES), F32)
    return pl.pallas_call(
        _prep_kernel,
        grid=(nblk,),
        in_specs=[main, prev, nxt, full((1, RW_COLS)), full((1, 2 * RW)), full((LORA, 2 * RW)),
                  full((1, 2 * RW)), full((LORA, 2 * RW)), full((LORA, RW)),
                  full((1, RW)), full((1, RW)), full((1, RW))],
        out_specs=[srow, srow, srow, srow2, srow2, srow2, srow2, row, row],
        out_shape=[sone, sone, sone, stwo, stwo, stwo, stwo, one, one],
        compiler_params=_cparams(("parallel",)),
        name="rwkv_prep",
    )(pr, pr, pr, mu.reshape(1, RW_COLS), decay0.reshape(1, 2 * RW), _block_diag2(decay_up).astype(BF16),
      iclr0.reshape(1, 2 * RW), _block_diag2(iclr_up).astype(BF16), gate_up.astype(BF16),
      k_k.reshape(1, RW), k_a.reshape(1, RW), r_k.reshape(1, RW))


def _scan_products():
    prods = []
    for t in range(CHUNK):
        for s in range(t):
            prods.append(("ab", t, s))
    for t in range(CHUNK):
        for s in range(t):
            prods.append(("ak", t, s))
    for t in range(CHUNK):
        for s in range(t + 1):
            prods.append(("rb", t, s))
    for t in range(CHUNK):
        for s in range(t + 1):
            prods.append(("rk", t, s))
    return prods


_PRODS = _scan_products()
_PROD_POS = {p: (16 * (n // 4) + SUBLANES * ((n % 4) // 2), LANES * (n % 2)) for n, p in enumerate(_PRODS)}
_N_PROD_ROWS = SUBLANES * len(_PRODS) // 2
_PROD_TILE = 1024
_NT = (((1,), (1,)), ((), ()))
_TN = (((0,), (0,)), ((), ()))


def _scan_chunk(c, n_chunks, rev, r_ref, v_ref, a_ref, w_ref, iw_ref, k_ref, b_ref, y_ref,
                s_ref, sb_ref, at_ref, rt_ref, bt_ref, kt_ref, ut_ref, sa_ref, sr_ref,
                gc_ref, lhs_ref, m_ref, ones_bd, diag):
    c16 = CHUNK * SUBLANES
    cn = (n_chunks - 1 - c) if rev else c
    base = pl.multiple_of(cn * c16, c16)

    def loc(tau):
        return ((CHUNK - 1 - tau) if rev else tau) * SUBLANES

    def rows(ref, tau):
        return ref[pl.ds(pl.multiple_of(base + loc(tau), SUBLANES), SUBLANES), :]

    def local(ref, tau):
        return ref[loc(tau):loc(tau) + SUBLANES, :]

    def pair_rows(ref, p):
        return ref[pl.ds(p, CHUNK, stride=SUBLANES), :]

    g = None
    ig = None
    for tau in range(CHUNK):
        sl = slice(loc(tau), loc(tau) + SUBLANES)
        a = rows(a_ref, tau)
        at_ref[sl, :] = a if g is None else a * g
        w = rows(w_ref, tau)
        iw = rows(iw_ref, tau)
        g = w if g is None else g * w
        ig = iw if ig is None else ig * iw
        rt_ref[sl, :] = rows(r_ref, tau) * g
        bt_ref[sl, :] = rows(b_ref, tau) * ig
        kt_ref[sl, :] = rows(k_ref, tau) * ig
    gc_ref[...] = g

    for p in range(PAIRS):
        ar = jnp.concatenate([pair_rows(at_ref, p), pair_rows(rt_ref, p)], axis=0)
        res = lax.dot_general(ar.astype(BF16), sb_ref[p], _NT, preferred_element_type=F32)
        sa_ref[pl.ds(p, CHUNK, stride=SUBLANES), :] = res[:CHUNK]
        sr_ref[pl.ds(p, CHUNK, stride=SUBLANES), :] = res[CHUNK:]

    srcs = {"a": at_ref, "r": rt_ref, "b": bt_ref, "k": kt_ref}

    def prod(n):
        kind, t, s = _PRODS[n]
        return local(srcs[kind[0]], t) * local(srcs[kind[1]], s)

    for q in range(len(_PRODS) // 4):
        left = jnp.concatenate([prod(4 * q), prod(4 * q + 2)], axis=0)
        right = jnp.concatenate([prod(4 * q + 1), prod(4 * q + 3)], axis=0)
        lhs_ref[16 * q:16 * (q + 1), 0:LANES] = left.astype(BF16)
        lhs_ref[16 * q:16 * (q + 1), LANES:2 * LANES] = right.astype(BF16)
    for q in range(_N_PROD_ROWS // _PROD_TILE):
        sl = slice(q * _PROD_TILE, (q + 1) * _PROD_TILE)
        m_ref[sl, :] = jnp.dot(lhs_ref[sl, :], ones_bd, preferred_element_type=F32)

    def coef(kind, t, s):
        r0, c0 = _PROD_POS[(kind, t, s)]
        return m_ref[r0:r0 + SUBLANES, c0:c0 + LANES]

    vs = [rows(v_ref, tau) for tau in range(CHUNK)]
    us = []
    for t in range(CHUNK):
        acc = local(sa_ref, t)
        for s in range(t):
            acc = acc + coef("ak", t, s) * vs[s]
        for s in range(t):
            acc = acc + coef("ab", t, s) * us[s]
        us.append(acc)
        ut_ref[loc(t):loc(t) + SUBLANES, :] = acc

    for t in range(CHUNK):
        acc = local(sr_ref, t)
        for s in range(t + 1):
            acc = acc + coef("rk", t, s) * vs[s] + coef("rb", t, s) * us[s]
        y_ref[pl.ds(pl.multiple_of(base + loc(t), SUBLANES), SUBLANES), :] = acc

    for p in range(PAIRS):
        gp = gc_ref[p:p + 1, :]
        bk = jnp.concatenate([pair_rows(bt_ref, p), pair_rows(kt_ref, p)], axis=0) * gp
        uv = jnp.concatenate([pair_rows(ut_ref, p),
                              v_ref[pl.ds(base + p, CHUNK, stride=SUBLANES), :]], axis=0)
        upd = lax.dot_general(uv, bk, _TN, preferred_element_type=F32)
        s_new = s_ref[p] * gp + jnp.where(diag, upd, 0.0)
        s_ref[p] = s_new
        sb_ref[p] = s_new.astype(BF16)


_SCAN_SCRATCH_PER_DIR = 12


def _scan_kernel(*refs, n_chunks):
    ins, outs, scr = refs[:14], refs[14:16], refs[16:]
    rf, vf, af, rb, vb, ab, wf, iwf, kf, bf, wb, iwb, kb, bb = ins
    fwd = (rf, vf, af, wf, iwf, kf, bf, outs[0]) + tuple(scr[:_SCAN_SCRATCH_PER_DIR])
    bwd = (rb, vb, ab, wb, iwb, kb, bb, outs[1]) + tuple(scr[_SCAN_SCRATCH_PER_DIR:])

    @pl.when(pl.program_id(0) == 0)
    def _():
        for st in (fwd, bwd):
            for ref in st[8:10]:
                ref[...] = jnp.zeros_like(ref)

    rr = lax.broadcasted_iota(jnp.int32, (2 * LANES, 2 * LANES), 0) >> 6
    cc = lax.broadcasted_iota(jnp.int32, (2 * LANES, 2 * LANES), 1) >> 6
    ones_bd = (rr == cc).astype(BF16)
    ri = lax.broadcasted_iota(jnp.int32, (LANES, LANES), 0) >> 6
    ci = lax.broadcasted_iota(jnp.int32, (LANES, LANES), 1) >> 6
    diag = ri == ci

    def body(c, carry):
        _scan_chunk(c, n_chunks, False, *fwd, ones_bd, diag)
        _scan_chunk(c, n_chunks, True, *bwd, ones_bd, diag)
        return carry

    lax.fori_loop(0, n_chunks, body, 0)


def _scan(r, v, a, w, iw, kd, bd, tb):
    t = r.shape[0] // SUBLANES
    nb = t // tb
    rows = tb * SUBLANES
    sf = pl.BlockSpec((rows, LANES), lambda i: (i, 0))
    sb = pl.BlockSpec((rows, LANES), lambda i: (nb - 1 - i, 0))
    df = pl.BlockSpec((None, rows, LANES), lambda i: (0, i, 0))
    db = pl.BlockSpec((None, rows, LANES), lambda i: (1, nb - 1 - i, 0))
    c16 = CHUNK * SUBLANES
    per_dir = ([pltpu.VMEM((PAIRS, LANES, LANES), F32), pltpu.VMEM((PAIRS, LANES, LANES), BF16)]
               + [pltpu.VMEM((c16, LANES), F32) for _ in range(7)]
               + [pltpu.VMEM((SUBLANES, LANES), F32),
                  pltpu.VMEM((_N_PROD_ROWS, 2 * LANES), BF16),
                  pltpu.VMEM((_N_PROD_ROWS, 2 * LANES), F32)])
    assert len(per_dir) == _SCAN_SCRATCH_PER_DIR
    out = jax.ShapeDtypeStruct((t * SUBLANES, LANES), F32)
    return pl.pallas_call(
        functools.partial(_scan_kernel, n_chunks=tb // CHUNK),
        grid=(nb,),
        in_specs=[sf, sf, sf, sb, sb, sb, df, df, df, df, db, db, db, db],
        out_specs=[sf, sb],
        out_shape=[out, out],
        scratch_shapes=per_dir + per_dir,
        compiler_params=_cparams(("arbitrary",)),
        name="rwkv_scan",
    )(r, v, a, r, v, a, w, iw, kd, bd, w, iw, kd, bd)


def _top4(logits):
    lane = lax.broadcasted_iota(jnp.int32, logits.shape, 1).astype(F32)
    vals, idxs = [], []
    l = logits
    for _ in range(TOP_K):
        m = jnp.max(l, axis=-1, keepdims=True)
        idx = jnp.min(jnp.where(l == m, lane, float(LANES)), axis=-1, keepdims=True)
        vals.append(m)
        idxs.append(idx)
        l = jnp.where(lane == idx, -jnp.inf, l)
    es = [jnp.exp(vk - vals[0]) for vk in vals]
    tot = es[0] + es[1] + es[2] + es[3]
    top_i = jnp.zeros(logits.shape, F32)
    gates = jnp.zeros(logits.shape, F32)
    for k in range(TOP_K):
        top_i = jnp.where(lane == float(k), idxs[k], top_i)
        gates = jnp.where(lane == float(k), es[k] / tot, gates)
    return top_i.astype(jnp.int32), gates


def _post_kernel(yf_ref, yb_ref, bonus_ref, g_ref, c_ref, cp_ref, cn_ref, h_ref, wo_ref,
                 lg_ref, lb_ref, cw_ref, os_ref, n1g_ref, n1b_ref, wr_ref, br_ref,
                 h1_ref, ti_ref, gt_ref, ocat_ref):
    i = pl.program_id(0)
    first = i == 0
    last = i == pl.num_programs(0) - 1
    ones = _head_ones2()
    inv_n = 1.0 / HEAD
    tm = h_ref.shape[0]

    for s in range(RW // LANES):
        cs = slice(s * LANES, (s + 1) * LANES)
        pair = pl.ds(s, tm, stride=SUBLANES)
        y = yf_ref[pair, :] + yb_ref[pair, :]
        mu = _head_sum(y, ones) * inv_n
        yc = y - mu
        var = _head_sum(yc * yc, ones) * inv_n
        yn = yc * lax.rsqrt(var + GN_EPS) * lg_ref[:, cs] + lb_ref[:, cs] + bonus_ref[:, cs]
        ocat_ref[:, cs] = (yn * g_ref[:, cs] * os_ref[:, cs]).astype(BF16)

    for s in range(CONV // LANES):
        cs = slice(s * LANES, (s + 1) * LANES)
        gs = slice(CONV + s * LANES, CONV + (s + 1) * LANES)
        hs = slice(2 * CONV + s * LANES, 2 * CONV + (s + 1) * LANES)
        u = c_ref[:, gs] * c_ref[:, hs]
        u_prev = jnp.where(first, 0.0, cp_ref[SUBLANES - 1:SUBLANES, gs] * cp_ref[SUBLANES - 1:SUBLANES, hs])
        u_next = jnp.where(last, 0.0, cn_ref[0:1, gs] * cn_ref[0:1, hs])
        up, un = _shift_rows(u, u_prev, u_next)
        conv = cw_ref[0:1, cs] * up + cw_ref[1:2, cs] * u + cw_ref[2:3, cs] * un
        oc = slice(RW + s * LANES, RW + (s + 1) * LANES)
        ocat_ref[:, oc] = (c_ref[:, cs] * conv * os_ref[:, oc]).astype(BF16)

    m = jnp.dot(ocat_ref[...], wo_ref[...], preferred_element_type=F32)
    h1 = _layer_norm(ALPHA_RES * h_ref[...] + m, n1g_ref[...], n1b_ref[...])
    h1_ref[...] = h1
    logits = jnp.dot(h1, wr_ref[...], precision=HI, preferred_element_type=F32) + br_ref[...]
    ti, gt = _top4(logits)
    ti_ref[...] = ti
    gt_ref[...] = gt


def _post(yf, yb, bonus, g, pc, h, w_out_b, lnx_g, lnx_b, conv_w, out_scale, ln1_g, ln1_b,
          w_router, b_router, tm):
    t = h.shape[0]
    nblk = t // tm
    main, prev, nxt = _halo_specs(tm, CONV_COLS, nblk)

    def full(shape):
        return pl.BlockSpec(shape, lambda i: (0,) * len(shape))

    row = lambda c: pl.BlockSpec((tm, c), lambda i: (i, 0))
    srow = pl.BlockSpec((tm * SUBLANES, LANES), lambda i: (i, 0))
    wr = jnp.zeros((D_MODEL, LANES), F32).at[:, :N_EXP].set(w_router)
    br = jnp.full((1, LANES), -1e30, F32).at[0, :N_EXP].set(b_router)
    return pl.pallas_call(
        _post_kernel,
        grid=(nblk,),
        in_specs=[srow, srow, row(RW), row(RW),
                  main, prev, nxt, row(D_MODEL), full((D_MODEL, D_MODEL)),
                  full((1, RW)), full((1, RW)), full((3, CONV)), full((1, D_MODEL)),
                  full((1, D_MODEL)), full((1, D_MODEL)), full((D_MODEL, LANES)), full((1, LANES))],
        out_specs=[row(D_MODEL), row(LANES), row(LANES)],
        out_shape=[jax.ShapeDtypeStruct((t, D_MODEL), F32),
                   jax.ShapeDtypeStruct((t, LANES), jnp.int32), jax.ShapeDtypeStruct((t, LANES), F32)],
        scratch_shapes=[pltpu.VMEM((tm, D_MODEL), BF16)],
        compiler_params=_cparams(("parallel",)),
        name="mixer_out",
    )(yf, yb, bonus, g, pc, pc, pc, h, w_out_b, lnx_g.reshape(1, RW), lnx_b.reshape(1, RW), conv_w,
      out_scale.reshape(1, D_MODEL), ln1_g.reshape(1, D_MODEL), ln1_b.reshape(1, D_MODEL), wr, br)


def _moe_kernel(ve_ref, vr_ref, vn_ref, tail_ref, tok_ref, x_hbm, wg_ref, wl_ref, bg_ref, bl_ref,
                wo_ref, bo_ref, y_hbm, xbuf, acc, wg_b, wl_b, wo_b, sem_in, sem_out):
    v = pl.program_id(0)
    f = pl.program_id(1)
    n128 = vn_ref[v]
    n_sub = (n128 + 1) >> 1
    row0 = pl.multiple_of(vr_ref[v], MOE_BLOCK)

    def in_copy(j):
        return pltpu.make_async_copy(x_hbm.at[pl.ds(tok_ref[0, j], 1), :],
                                     xbuf.at[pl.ds(j, 1), :], sem_in)

    def for_rows(n_groups, fn):
        def group(gi, c):
            for u in range(SUBLANES):
                fn(gi * SUBLANES + u)
            return c
        lax.fori_loop(0, n_groups, group, 0)

    def out_copy(j):
        r = pl.multiple_of(j * MOE_BLOCK, MOE_BLOCK)
        return pltpu.make_async_copy(acc.at[pl.ds(r, MOE_BLOCK), :],
                                     y_hbm.at[pl.ds(row0 + r, MOE_BLOCK), :], sem_out)

    @pl.when(jnp.logical_and(f == 0, n128 > 0))
    def _():
        n_groups = n_sub * (MOE_SUB // SUBLANES)
        for_rows(n_groups, lambda j: in_copy(j).start())

        def init(j, c):
            rs = pl.ds(pl.multiple_of(j * MOE_SUB, MOE_SUB), MOE_SUB)
            acc[rs, :] = jnp.broadcast_to(bo_ref[...], (MOE_SUB, D_MODEL))
            return c

        lax.fori_loop(0, n_sub, init, 0)
        for_rows(n_groups, lambda j: in_copy(j).wait())

    @pl.when(n128 > 0)
    def _():
        wg_b[...] = wg_ref[...].astype(BF16)
        wl_b[...] = wl_ref[...].astype(BF16)
        wo_b[...] = wo_ref[...].astype(BF16)

        def sub(j, c):
            rs = pl.ds(pl.multiple_of(j * MOE_SUB, MOE_SUB), MOE_SUB)
            x = xbuf[rs, :].astype(BF16)
            hg = jnp.dot(x, wg_b[...], preferred_element_type=F32) + bg_ref[...]
            hl = jnp.dot(x, wl_b[...], preferred_element_type=F32) + bl_ref[...]
            glu = jnp.minimum(hg, SWIGLU_LIMIT)
            lin = jnp.clip(hl, -SWIGLU_LIMIT, SWIGLU_LIMIT)
            act = glu * jax.nn.sigmoid(SWIGLU_ALPHA * glu) * (lin + 1.0)
            acc[rs, :] += jnp.dot(act.astype(BF16), wo_b[...], preferred_element_type=F32)
            return c

        lax.fori_loop(0, n_sub, sub, 0)

    @pl.when(jnp.logical_and(f == pl.num_programs(1) - 1, n128 > 0))
    def _():
        lax.fori_loop(0, n128, lambda j, c: (out_copy(j).start(), c)[1], 0)
        lax.fori_loop(0, n128, lambda j, c: (out_copy(j).wait(), c)[1], 0)

    @pl.when(jnp.logical_and(v == pl.num_programs(0) - 1, f == pl.num_programs(1) - 1))
    def _():
        acc[0:MOE_BLOCK, :] = jnp.zeros((MOE_BLOCK, D_MODEL), F32)

        def tail_copy(j):
            r = pl.multiple_of(tail_ref[0] + j * MOE_BLOCK, MOE_BLOCK)
            return pltpu.make_async_copy(acc.at[0:MOE_BLOCK, :], y_hbm.at[pl.ds(r, MOE_BLOCK), :], sem_out)

        lax.fori_loop(0, tail_ref[1], lambda j, c: (tail_copy(j).start(), c)[1], 0)
        lax.fori_loop(0, tail_ref[1], lambda j, c: (tail_copy(j).wait(), c)[1], 0)


def _moe_experts(h, vis_tok, vis_e, vis_row, vis_n, tail, layer, w_in, b_in, w_out, b_out, p_rows):
    nv = vis_e.shape[0]
    nf = D_EXP // MOE_TF

    def tile(v, f, vn):
        return jnp.where(vn[v] > 0, f, nf - 1)

    grid_spec = pltpu.PrefetchScalarGridSpec(
        num_scalar_prefetch=4,
        grid=(nv, nf),
        in_specs=[
            pl.BlockSpec((None, 1, MOE_TMX), lambda v, f, ve, vr, vn, tl: (v, 0, 0),
                         memory_space=pltpu.SMEM),
            pl.BlockSpec(memory_space=pl.ANY),
            pl.BlockSpec((None, None, D_MODEL, MOE_TF), lambda v, f, ve, vr, vn, tl: (layer, ve[v], 0, tile(v, f, vn))),
            pl.BlockSpec((None, None, D_MODEL, MOE_TF), lambda v, f, ve, vr, vn, tl: (layer, ve[v], 0, nf + tile(v, f, vn))),
            pl.BlockSpec((None, None, 1, MOE_TF), lambda v, f, ve, vr, vn, tl: (layer, ve[v], 0, tile(v, f, vn))),
            pl.BlockSpec((None, None, 1, MOE_TF), lambda v, f, ve, vr, vn, tl: (layer, ve[v], 0, nf + tile(v, f, vn))),
            pl.BlockSpec((None, None, MOE_TF, D_MODEL), lambda v, f, ve, vr, vn, tl: (layer, ve[v], tile(v, f, vn), 0)),
            pl.BlockSpec((None, None, 1, D_MODEL), lambda v, f, ve, vr, vn, tl: (layer, ve[v], 0, 0)),
        ],
        out_specs=pl.BlockSpec(memory_space=pl.ANY),
        scratch_shapes=[pltpu.VMEM((MOE_TMX, D_MODEL), F32), pltpu.VMEM((MOE_TMX, D_MODEL), F32),
                        pltpu.VMEM((D_MODEL, MOE_TF), BF16), pltpu.VMEM((D_MODEL, MOE_TF), BF16),
                        pltpu.VMEM((MOE_TF, D_MODEL), BF16),
                        pltpu.SemaphoreType.DMA(()), pltpu.SemaphoreType.DMA(())],
    )
    return pl.pallas_call(
        _moe_kernel,
        grid_spec=grid_spec,
        out_shape=jax.ShapeDtypeStruct((p_rows, D_MODEL), F32),
        compiler_params=_cparams(("arbitrary", "arbitrary")),
        name="moe_experts",
    )(vis_e, vis_row, vis_n, tail, vis_tok, h, w_in, w_in, b_in.reshape(DEPTH, N_EXP, 1, 2 * D_EXP),
      b_in.reshape(DEPTH, N_EXP, 1, 2 * D_EXP), w_out, b_out.reshape(DEPTH, N_EXP, 1, D_MODEL))


def _route(top_i, t):
    a = t * TOP_K
    flat_e = top_i.reshape(-1)
    onehot = (flat_e[:, None] == jnp.arange(N_EXP, dtype=jnp.int32)[None, :]).astype(jnp.int32)
    csum = jnp.cumsum(onehot, axis=0)
    rank = jnp.take_along_axis(csum, flat_e[:, None], axis=1)[:, 0] - 1
    counts = csum[-1]
    padded = (counts + MOE_BLOCK - 1) // MOE_BLOCK * MOE_BLOCK
    pends = jnp.cumsum(padded)
    pstarts = pends - padded
    dest = pstarts[flat_e] + rank

    p_rows = (a + N_EXP * (MOE_BLOCK - 1) + MOE_BLOCK - 1) // MOE_BLOCK * MOE_BLOCK
    p_alloc = p_rows + MOE_TMX
    row_tok = jnp.zeros((p_alloc,), jnp.int32).at[dest].set(jnp.arange(a, dtype=jnp.int32) // TOP_K)

    n_vis = N_EXP + (p_rows + MOE_TMX - 1) // MOE_TMX
    per_e = (padded + MOE_TMX - 1) // MOE_TMX
    vend = jnp.cumsum(per_e)
    vstart = vend - per_e
    vid = jnp.arange(n_vis, dtype=jnp.int32)
    valid = vid < vend[-1]
    e_of = jnp.clip(jnp.searchsorted(vend, vid, side="right"), 0, N_EXP - 1).astype(jnp.int32)
    e_last = jnp.clip(jnp.searchsorted(vend, vend[-1] - 1, side="right"), 0, N_EXP - 1).astype(jnp.int32)
    e_of = jnp.where(valid, e_of, e_last)
    local = vid - vstart[e_of]
    vis_row = jnp.where(valid, pstarts[e_of] + local * MOE_TMX, 0).astype(jnp.int32)
    vis_rows = jnp.where(valid, jnp.clip(padded[e_of] - local * MOE_TMX, 0, MOE_TMX), 0)
    vis_n = (vis_rows // MOE_BLOCK).astype(jnp.int32)
    tail = jnp.stack([pends[-1], (p_rows - pends[-1]) // MOE_BLOCK]).astype(jnp.int32)
    vis_tok = row_tok[vis_row[:, None] + jnp.arange(MOE_TMX, dtype=jnp.int32)[None, :]]
    return dest, vis_tok.reshape(n_vis, 1, MOE_TMX), e_of, vis_row, vis_n, tail, p_rows


def _combine_kernel(pos_ref, y_hbm, gt_ref, h_ref, g_ref, b_ref, o_ref, ob_ref, buf, sem):
    tm = h_ref.shape[0]

    def copy(t, k):
        return pltpu.make_async_copy(y_hbm.at[pl.ds(pos_ref[0, t * TOP_K + k], 1), :],
                                     buf.at[k, pl.ds(t, 1), :], sem)

    def for_tokens(fn):
        def group(gi, c):
            for u in range(SUBLANES):
                for k in range(TOP_K):
                    fn(gi * SUBLANES + u, k)
            return c
        lax.fori_loop(0, tm // SUBLANES, group, 0)

    for_tokens(lambda t, k: copy(t, k).start())
    for_tokens(lambda t, k: copy(t, k).wait())
    gt = gt_ref[...]
    f = (gt[:, 0:1] * buf[0] + gt[:, 1:2] * buf[1] + gt[:, 2:3] * buf[2] + gt[:, 3:4] * buf[3])
    h2 = _layer_norm(ALPHA_RES * h_ref[...] + f, g_ref[...], b_ref[...])
    o_ref[...] = h2
    ob_ref[...] = h2.astype(BF16)


def _combine(yb, pos, gates, h, g, b, tm):
    t, d = h.shape
    nt = t // tm
    row = pl.BlockSpec((tm, d), lambda i: (i, 0))
    vec = pl.BlockSpec((1, d), lambda i: (0, 0))
    return pl.pallas_call(
        _combine_kernel,
        grid=(nt,),
        in_specs=[pl.BlockSpec((None, 1, tm * TOP_K), lambda i: (i, 0, 0), memory_space=pltpu.SMEM),
                  pl.BlockSpec(memory_space=pl.ANY),
                  pl.BlockSpec((tm, LANES), lambda i: (i, 0)), row, vec, vec],
        out_specs=[row, row],
        out_shape=[jax.ShapeDtypeStruct((t, d), F32), jax.ShapeDtypeStruct((t, d), BF16)],
        scratch_shapes=[pltpu.VMEM((TOP_K, tm, d), F32), pltpu.SemaphoreType.DMA(())],
        compiler_params=_cparams(("arbitrary",)),
        name="moe_combine",
    )(pos.reshape(nt, 1, tm * TOP_K), yb, gates, h, g.reshape(1, d), b.reshape(1, d))


def kernel(x, meta_tokens, ln_in_g, ln_in_b, w_in, mu_shift, decay0, decay_up, iclr0, iclr_up,
           gate_up, k_k, k_a, r_k, lnx_g, lnx_b, conv_w, out_scale, w_out, ln1_g, ln1_b,
           w_router, b_router, w_exp_in, b_exp_in, w_exp_out, b_exp_out, ln2_g, ln2_b):
    bsz, seq, d = x.shape
    assert bsz == 1 and d == D_MODEL
    t = N_META + seq
    tm = _largest_divisor(t, 16, 512)
    tm_post = _largest_divisor(t, 16, 320)
    tb = _largest_divisor(t, CHUNK, 160)

    h0 = jnp.concatenate([meta_tokens.astype(x.dtype), x[0]], axis=0)
    h, hb = _ln_in(h0, ln_in_g, ln_in_b, tm)
    for l in range(DEPTH):
        w_in_b = w_in[l].astype(BF16)
        pr = _matmul(hb, w_in_b[:, :RW_COLS], tm, RW_COLS // 3)
        pc = _matmul(hb, w_in_b[:, RW_COLS:], tm, CONV_COLS // 3)
        r, v, a, w, iw, kd, bd, bonus, g = _prep(pr, mu_shift[l], decay0[l], decay_up[l], iclr0[l],
                                                 iclr_up[l], gate_up[l], k_k[l], k_a[l], r_k[l], tb)
        yf, yb_ = _scan(r, v, a, w, iw, kd, bd, tb)
        h1, top_i, gates = _post(yf, yb_, bonus, g, pc, h, w_out[l].astype(BF16), lnx_g[l], lnx_b[l],
                                      conv_w[l], out_scale[l], ln1_g[l], ln1_b[l],
                                      w_router[l], b_router[l], tm_post)
        pos, vis_tok, vis_e, vis_row, vis_n, tail, p_rows = _route(top_i[:, :TOP_K], t)
        yb = _moe_experts(h1, vis_tok, vis_e, vis_row, vis_n, tail, l, w_exp_in, b_exp_in, w_exp_out,
                          b_exp_out, p_rows)
        h, hb = _combine(yb, pos, gates, h1, ln2_g[l], ln2_b[l], tm_post)
    return h[N_META:].reshape(bsz, seq, d)
```

```python
import functools

import jax
import jax.numpy as jnp
from jax import lax
from jax.experimental import pallas as pl
from jax.experimental.pallas import tpu as pltpu

F32 = jnp.float32
BF16 = jnp.bfloat16
HI = lax.Precision.HIGHEST

D_MODEL = 2048
N_META = 16
HEAD = 64
RW = 1024
CONV = 1024
LORA = 128
RW_COLS = 3 * RW + 3 * LORA
CONV_COLS = 3 * CONV
N_EXP = 32
TOP_K = 4
D_EXP = 2048
MOE_BLOCK = 128
SWIGLU_LIMIT = 7.0
SWIGLU_ALPHA = 1.702
DEPTH = 2
ALPHA_RES = (2 * DEPTH) ** 0.25
LN_EPS = 1e-5
GN_EPS = 64e-5

LANES = 128
SUBLANES = 8
VMEM_LIMIT = 56 * 1024 * 1024

CHUNK = 16
PAIRS = RW // LANES
MOE_TMX = 1536
MOE_SUB = 256
MOE_TF = 256


def _largest_divisor(n, multiple, cap):
    best = None
    for d in range(multiple, cap + 1, multiple):
        if n % d == 0:
            best = d
    assert best is not None, (n, multiple, cap)
    return best


def _cparams(sem):
    return pltpu.CompilerParams(dimension_semantics=sem, vmem_limit_bytes=VMEM_LIMIT)


def _layer_norm(x, g, b):
    mu = jnp.mean(x, axis=-1, keepdims=True)
    xc = x - mu
    var = jnp.mean(xc * xc, axis=-1, keepdims=True)
    return xc * lax.rsqrt(var + LN_EPS) * g + b


def _head_ones2():
    r = (lax.broadcasted_iota(jnp.int32, (2 * LANES, LANES), 0) >> 6) & 1
    c = lax.broadcasted_iota(jnp.int32, (2 * LANES, LANES), 1) >> 6
    return (r == c).astype(BF16)


def _split_bf16(x):
    hi = x.astype(BF16)
    return hi, (x - hi.astype(F32)).astype(BF16)


def _head_sum(x, ones2):
    hi, lo = _split_bf16(x)
    return jnp.dot(jnp.concatenate([hi, lo], axis=1), ones2, preferred_element_type=F32)


def _ln_in_kernel(x_ref, g_ref, b_ref, h_ref, hb_ref):
    y = _layer_norm(x_ref[...], g_ref[...], b_ref[...])
    h_ref[...] = y
    hb_ref[...] = y.astype(BF16)


def _ln_in(x, g, b, tm):
    t, d = x.shape
    row = pl.BlockSpec((tm, d), lambda i: (i, 0))
    vec = pl.BlockSpec((1, d), lambda i: (0, 0))
    return pl.pallas_call(
        _ln_in_kernel,
        grid=(t // tm,),
        in_specs=[row, vec, vec],
        out_specs=[row, row],
        out_shape=[jax.ShapeDtypeStruct((t, d), F32), jax.ShapeDtypeStruct((t, d), BF16)],
        compiler_params=_cparams(("parallel",)),
        name="ln_in",
    )(x, g.reshape(1, d), b.reshape(1, d))


def _mm_kernel(a_ref, w_ref, o_ref):
    o_ref[...] = jnp.dot(a_ref[...], w_ref[...], preferred_element_type=F32)


def _matmul(a, w, tm, tn):
    t, k = a.shape
    n = w.shape[1]
    return pl.pallas_call(
        _mm_kernel,
        grid=(n // tn, t // tm),
        in_specs=[pl.BlockSpec((tm, k), lambda j, i: (i, 0)),
                  pl.BlockSpec((k, tn), lambda j, i: (0, j))],
        out_specs=pl.BlockSpec((tm, tn), lambda j, i: (i, j)),
        out_shape=jax.ShapeDtypeStruct((t, n), F32),
        compiler_params=_cparams(("parallel", "parallel")),
        name="in_proj",
    )(a, w)


def _shift_rows(x, prev_row, next_row):
    tm = x.shape[0]
    rows = lax.broadcasted_iota(jnp.int32, x.shape, 0)
    xp = jnp.where(rows == 0, prev_row, pltpu.roll(x, 1, axis=0))
    xn = jnp.where(rows == tm - 1, next_row, pltpu.roll(x, tm - 1, axis=0))
    return xp, xn


def _halo_specs(tm, cols, nblk):
    per = tm // SUBLANES
    main = pl.BlockSpec((tm, cols), lambda i: (i, 0))
    prev = pl.BlockSpec((SUBLANES, cols), lambda i: (jnp.maximum(i * per - 1, 0), 0))
    nxt = pl.BlockSpec((SUBLANES, cols), lambda i: (jnp.minimum((i + 1) * per, nblk * per - 1), 0))
    return main, prev, nxt


def _prep_kernel(p_ref, pp_ref, pn_ref, mu_ref, d0_ref, du_ref, i0_ref, iu_ref, gu_ref,
                 kk_ref, ka_ref, rk_ref,
                 r_ref, v_ref, a_ref, w_ref, iw_ref, kd_ref, bd_ref, bonus_ref, g_ref):
    i = pl.program_id(0)
    first = i == 0
    last = i == pl.num_programs(0) - 1
    ones = _head_ones2()
    tm = p_ref.shape[0]

    def mixed(c0):
        cs = slice(c0, c0 + LANES)
        x = p_ref[:, cs]
        prev_row = jnp.where(first, 0.0, pp_ref[SUBLANES - 1:SUBLANES, cs])
        next_row = jnp.where(last, 0.0, pn_ref[0:1, cs])
        xp, xn = _shift_rows(x, prev_row, next_row)
        return x + (0.5 * (xp + xn) - x) * mu_ref[:, cs]

    tanh_cw = jnp.tanh(mixed(3 * RW)).astype(BF16)
    ca = mixed(3 * RW + LORA).astype(BF16)
    sig_cg = jax.nn.sigmoid(mixed(3 * RW + 2 * LORA)).astype(BF16)

    for s in range(RW // LANES):
        cs = slice(s * LANES, (s + 1) * LANES)
        r = mixed(s * LANES)
        k = mixed(RW + s * LANES)
        v = mixed(2 * RW + s * LANES)
        kk = k * kk_ref[:, cs]
        norm = jnp.sqrt(_head_sum(kk * kk, ones))
        kk = kk / jnp.maximum(norm, 1e-12)
        pair = pl.ds(s, tm, stride=SUBLANES)
        r_ref[pair, :] = r
        v_ref[pair, :] = v
        a_ref[pair, :] = -kk
        bonus_ref[:, cs] = _head_sum(r * k * rk_ref[:, cs], ones) * v
        g_ref[:, cs] = jnp.dot(sig_cg, gu_ref[:, cs], preferred_element_type=F32)
        for d in range(2):
            ds_ = slice(d * RW + s * LANES, d * RW + (s + 1) * LANES)
            z = d0_ref[:, ds_] + jnp.dot(tanh_cw, du_ref[:, ds_], preferred_element_type=F32)
            w_log = -(jnp.maximum(-z, 0.0) + jnp.log(1.0 + jnp.exp(-jnp.abs(z)))) - 0.5
            e1 = jnp.exp(w_log)
            w_ref[d, pair, :] = jnp.exp(-e1)
            iw_ref[d, pair, :] = jnp.exp(e1)
            a_icl = jax.nn.sigmoid(i0_ref[:, ds_] + jnp.dot(ca, iu_ref[:, ds_],
                                                            preferred_element_type=F32))
            kd_ref[d, pair, :] = k * (1.0 + (a_icl - 1.0) * ka_ref[:, cs])
            bd_ref[d, pair, :] = kk * a_icl


def _block_diag2(m):
    z = jnp.zeros_like(m[0])
    return jnp.concatenate([jnp.concatenate([m[0], z], axis=1),
                            jnp.concatenate([z, m[1]], axis=1)], axis=0)


def _prep(pr, mu, decay0, decay_up, iclr0, iclr_up, gate_up, k_k, k_a, r_k, tm):
    t = pr.shape[0]
    nblk = t // tm
    main, prev, nxt = _halo_specs(tm, RW_COLS, nblk)

    def full(shape):
        return pl.BlockSpec(shape, lambda i: (0,) * len(shape))

    row = pl.BlockSpec((tm, RW), lambda i: (i, 0))
    srow = pl.BlockSpec((tm * SUBLANES, LANES), lambda i: (i, 0))
    srow2 = pl.BlockSpec((2, tm * SUBLANES, LANES), lambda i: (0, i, 0))
    one = jax.ShapeDtypeStruct((t, RW), F32)
    sone = jax.ShapeDtypeStruct((t * SUBLANES, LANES), F32)
    stwo = jax.ShapeDtypeStruct((2, t * SUBLANES, LANES), F32)
    return pl.pallas_call(
        _prep_kernel,
        grid=(nblk,),
        in_specs=[main, prev, nxt, full((1, RW_COLS)), full((1, 2 * RW)), full((LORA, 2 * RW)),
                  full((1, 2 * RW)), full((LORA, 2 * RW)), full((LORA, RW)),
                  full((1, RW)), full((1, RW)), full((1, RW))],
        out_specs=[srow, srow, srow, srow2, srow2, srow2, srow2, row, row],
        out_shape=[sone, sone, sone, stwo, stwo, stwo, stwo, one, one],
        compiler_params=_cparams(("parallel",)),
        name="rwkv_prep",
    )(pr, pr, pr, mu.reshape(1, RW_COLS), decay0.reshape(1, 2 * RW), _block_diag2(decay_up).astype(BF16),
      iclr0.reshape(1, 2 * RW), _block_diag2(iclr_up).astype(BF16), gate_up.astype(BF16),
      k_k.reshape(1, RW), k_a.reshape(1, RW), r_k.reshape(1, RW))


def _scan_products():
    prods = []
    for t in range(CHUNK):
        for s in range(t):
            prods.append(("ab", t, s))
    for t in range(CHUNK):
        for s in range(t):
            prods.append(("ak", t, s))
    for t in range(CHUNK):
        for s in range(t + 1):
            prods.append(("rb", t, s))
    for t in range(CHUNK):
        for s in range(t + 1):
            prods.append(("rk", t, s))
    return prods


_PRODS = _scan_products()
_PROD_POS = {p: (16 * (n // 4) + SUBLANES * ((n % 4) // 2), LANES * (n % 2)) for n, p in enumerate(_PRODS)}
_N_PROD_ROWS = SUBLANES * len(_PRODS) // 2
_PROD_TILE = 1024
_NT = (((1,), (1,)), ((), ()))
_TN = (((0,), (0,)), ((), ()))


class _ChunkIndex:
    def __init__(self, c, n_chunks, rev):
        c16 = CHUNK * SUBLANES
        self.rev = rev
        self.base = pl.multiple_of(((n_chunks - 1 - c) if rev else c) * c16, c16)

    def loc(self, tau):
        return ((CHUNK - 1 - tau) if self.rev else tau) * SUBLANES

    def rows(self, ref, tau):
        return ref[pl.ds(pl.multiple_of(self.base + self.loc(tau), SUBLANES), SUBLANES), :]

    def local(self, ref, tau):
        return ref[self.loc(tau):self.loc(tau) + SUBLANES, :]


def _pair_rows(ref, p):
    return ref[pl.ds(p, CHUNK, stride=SUBLANES), :]


def _scan_apply_state(sb_ref, at_ref, rt_ref, sa_ref, sr_ref):
    for p in range(PAIRS):
        ar = jnp.concatenate([_pair_rows(at_ref, p), _pair_rows(rt_ref, p)], axis=0)
        res = lax.dot_general(ar.astype(BF16), sb_ref[p], _NT, preferred_element_type=F32)
        sa_ref[pl.ds(p, CHUNK, stride=SUBLANES), :] = res[:CHUNK]
        sr_ref[pl.ds(p, CHUNK, stride=SUBLANES), :] = res[CHUNK:]


def _scan_prepare(ix, r_ref, a_ref, w_ref, iw_ref, k_ref, b_ref,
                  at_ref, rt_ref, bt_ref, kt_ref, gc_ref, lhs_ref, m_ref, ones_bd, between):
    g = None
    ig = None
    for tau in range(CHUNK):
        sl = slice(ix.loc(tau), ix.loc(tau) + SUBLANES)
        a = ix.rows(a_ref, tau)
        at_ref[sl, :] = a if g is None else a * g
        w = ix.rows(w_ref, tau)
        iw = ix.rows(iw_ref, tau)
        g = w if g is None else g * w
        ig = iw if ig is None else ig * iw
        rt_ref[sl, :] = ix.rows(r_ref, tau) * g
        bt_ref[sl, :] = ix.rows(b_ref, tau) * ig
        kt_ref[sl, :] = ix.rows(k_ref, tau) * ig
    gc_ref[...] = g
    between()

    srcs = {"a": at_ref, "r": rt_ref, "b": bt_ref, "k": kt_ref}

    def prod(n):
        kind, t, s = _PRODS[n]
        return ix.local(srcs[kind[0]], t) * ix.local(srcs[kind[1]], s)

    for q in range(len(_PRODS) // 4):
        left = jnp.concatenate([prod(4 * q), prod(4 * q + 2)], axis=0)
        right = jnp.concatenate([prod(4 * q + 1), prod(4 * q + 3)], axis=0)
        lhs_ref[16 * q:16 * (q + 1), 0:LANES] = left.astype(BF16)
        lhs_ref[16 * q:16 * (q + 1), LANES:2 * LANES] = right.astype(BF16)
    for q in range(_N_PROD_ROWS // _PROD_TILE):
        sl = slice(q * _PROD_TILE, (q + 1) * _PROD_TILE)
        m_ref[sl, :] = jnp.dot(lhs_ref[sl, :], ones_bd, preferred_element_type=F32)


def _scan_advance(ix, v_ref, y_ref, s_ref, sb_ref, at_ref, rt_ref, bt_ref, kt_ref, gc_ref, m_ref,
                  ut_ref, sa_ref, sr_ref, diag):
    loc, rows, local = ix.loc, ix.rows, ix.local

    def coef(kind, t, s):
        r0, c0 = _PROD_POS[(kind, t, s)]
        return m_ref[r0:r0 + SUBLANES, c0:c0 + LANES]

    us = []
    for t in range(CHUNK):
        acc = local(sa_ref, t)
        for s in range(t):
            acc = acc + coef("ak", t, s) * rows(v_ref, s)
        for s in range(t):
            acc = acc + coef("ab", t, s) * us[s]
        us.append(acc)
        ut_ref[loc(t):loc(t) + SUBLANES, :] = acc

    for t in range(CHUNK):
        acc = local(sr_ref, t)
        for s in range(t + 1):
            acc = acc + coef("rk", t, s) * rows(v_ref, s) + coef("rb", t, s) * us[s]
        y_ref[pl.ds(pl.multiple_of(ix.base + loc(t), SUBLANES), SUBLANES), :] = acc

    for p in range(PAIRS):
        gp = gc_ref[p:p + 1, :]
        bk = jnp.concatenate([_pair_rows(bt_ref, p), _pair_rows(kt_ref, p)], axis=0) * gp
        uv = jnp.concatenate([_pair_rows(ut_ref, p),
                              v_ref[pl.ds(ix.base + p, CHUNK, stride=SUBLANES), :]], axis=0)
        upd = lax.dot_general(uv, bk, _TN, preferred_element_type=F32)
        s_new = s_ref[p] * gp + jnp.where(diag, upd, 0.0)
        s_ref[p] = s_new
        sb_ref[p] = s_new.astype(BF16)


_SCAN_SCRATCH_PER_DIR = 12


def _scan_kernel(*refs, n_chunks):
    ins, outs, scr = refs[:14], refs[14:16], refs[16:]
    rf, vf, af, rb, vb, ab, wf, iwf, kf, bf, wb, iwb, kb, bb = ins
    dirs = ((False, (rf, af, wf, iwf, kf, bf), vf, outs[0], scr[:_SCAN_SCRATCH_PER_DIR]),
            (True, (rb, ab, wb, iwb, kb, bb), vb, outs[1], scr[_SCAN_SCRATCH_PER_DIR:]))

    @pl.when(pl.program_id(0) == 0)
    def _():
        for _, _, _, _, st in dirs:
            for ref in st[0:2]:
                ref[...] = jnp.zeros_like(ref)

    rr = lax.broadcasted_iota(jnp.int32, (2 * LANES, 2 * LANES), 0) >> 6
    cc = lax.broadcasted_iota(jnp.int32, (2 * LANES, 2 * LANES), 1) >> 6
    ones_bd = (rr == cc).astype(BF16)
    ri = lax.broadcasted_iota(jnp.int32, (LANES, LANES), 0) >> 6
    ci = lax.broadcasted_iota(jnp.int32, (LANES, LANES), 1) >> 6
    diag = ri == ci

    def body(c, carry):
        for rev, ops, v_ref, y_ref, st in dirs:
            s, sb, at, rt, bt, kt, gc, m, lhs, ut, sa, sr = st
            ix = _ChunkIndex(c, n_chunks, rev)
            _scan_prepare(ix, *ops, at, rt, bt, kt, gc, lhs, m, ones_bd,
                          between=lambda: _scan_apply_state(sb, at, rt, sa, sr))
            _scan_advance(ix, v_ref, y_ref, s, sb, at, rt, bt, kt, gc, m, ut, sa, sr, diag)
        return carry

    lax.fori_loop(0, n_chunks, body, 0)


def _scan(r, v, a, w, iw, kd, bd, tb):
    t = r.shape[0] // SUBLANES
    nb = t // tb
    rows = tb * SUBLANES
    sf = pl.BlockSpec((rows, LANES), lambda i: (i, 0))
    sb = pl.BlockSpec((rows, LANES), lambda i: (nb - 1 - i, 0))
    df = pl.BlockSpec((None, rows, LANES), lambda i: (0, i, 0))
    db = pl.BlockSpec((None, rows, LANES), lambda i: (1, nb - 1 - i, 0))
    c16 = CHUNK * SUBLANES
    per_dir = ([pltpu.VMEM((PAIRS, LANES, LANES), F32), pltpu.VMEM((PAIRS, LANES, LANES), BF16)]
               + [pltpu.VMEM((c16, LANES), F32) for _ in range(4)]
               + [pltpu.VMEM((SUBLANES, LANES), F32),
                  pltpu.VMEM((_N_PROD_ROWS, 2 * LANES), F32),
                  pltpu.VMEM((_N_PROD_ROWS, 2 * LANES), BF16)]
               + [pltpu.VMEM((c16, LANES), F32) for _ in range(3)])
    assert len(per_dir) == _SCAN_SCRATCH_PER_DIR
    out = jax.ShapeDtypeStruct((t * SUBLANES, LANES), F32)
    return pl.pallas_call(
        functools.partial(_scan_kernel, n_chunks=tb // CHUNK),
        grid=(nb,),
        in_specs=[sf, sf, sf, sb, sb, sb, df, df, df, df, db, db, db, db],
        out_specs=[sf, sb],
        out_shape=[out, out],
        scratch_shapes=per_dir + per_dir,
        compiler_params=_cparams(("arbitrary",)),
        name="rwkv_scan",
    )(r, v, a, r, v, a, w, iw, kd, bd, w, iw, kd, bd)


def _top4(logits):
    lane = lax.broadcasted_iota(jnp.int32, logits.shape, 1).astype(F32)
    vals, idxs = [], []
    l = logits
    for _ in range(TOP_K):
        m = jnp.max(l, axis=-1, keepdims=True)
        idx = jnp.min(jnp.where(l == m, lane, float(LANES)), axis=-1, keepdims=True)
        vals.append(m)
        idxs.append(idx)
        l = jnp.where(lane == idx, -jnp.inf, l)
    es = [jnp.exp(vk - vals[0]) for vk in vals]
    tot = es[0] + es[1] + es[2] + es[3]
    top_i = jnp.zeros(logits.shape, F32)
    gates = jnp.zeros(logits.shape, F32)
    for k in range(TOP_K):
        top_i = jnp.where(lane == float(k), idxs[k], top_i)
        gates = jnp.where(lane == float(k), es[k] / tot, gates)
    return top_i.astype(jnp.int32), gates


def _post_kernel(yf_ref, yb_ref, bonus_ref, g_ref, c_ref, cp_ref, cn_ref, h_ref, wo_ref,
                 lg_ref, lb_ref, cw_ref, os_ref, n1g_ref, n1b_ref, wr_ref, br_ref,
                 h1_ref, ti_ref, gt_ref, ocat_ref):
    i = pl.program_id(0)
    first = i == 0
    last = i == pl.num_programs(0) - 1
    ones = _head_ones2()
    inv_n = 1.0 / HEAD
    tm = h_ref.shape[0]

    for s in range(RW // LANES):
        cs = slice(s * LANES, (s + 1) * LANES)
        pair = pl.ds(s, tm, stride=SUBLANES)
        y = yf_ref[pair, :] + yb_ref[pair, :]
        mu = _head_sum(y, ones) * inv_n
        yc = y - mu
        var = _head_sum(yc * yc, ones) * inv_n
        yn = yc * lax.rsqrt(var + GN_EPS) * lg_ref[:, cs] + lb_ref[:, cs] + bonus_ref[:, cs]
        ocat_ref[:, cs] = (yn * g_ref[:, cs] * os_ref[:, cs]).astype(BF16)

    for s in range(CONV // LANES):
        cs = slice(s * LANES, (s + 1) * LANES)
        gs = slice(CONV + s * LANES, CONV + (s + 1) * LANES)
        hs = slice(2 * CONV + s * LANES, 2 * CONV + (s + 1) * LANES)
        u = c_ref[:, gs] * c_ref[:, hs]
        u_prev = jnp.where(first, 0.0, cp_ref[SUBLANES - 1:SUBLANES, gs] * cp_ref[SUBLANES - 1:SUBLANES, hs])
        u_next = jnp.where(last, 0.0, cn_ref[0:1, gs] * cn_ref[0:1, hs])
        up, un = _shift_rows(u, u_prev, u_next)
        conv = cw_ref[0:1, cs] * up + cw_ref[1:2, cs] * u + cw_ref[2:3, cs] * un
        oc = slice(RW + s * LANES, RW + (s + 1) * LANES)
        ocat_ref[:, oc] = (c_ref[:, cs] * conv * os_ref[:, oc]).astype(BF16)

    m = jnp.dot(ocat_ref[...], wo_ref[...], preferred_element_type=F32)
    h1 = _layer_norm(ALPHA_RES * h_ref[...] + m, n1g_ref[...], n1b_ref[...])
    h1_ref[...] = h1
    logits = jnp.dot(h1, wr_ref[...], precision=HI, preferred_element_type=F32) + br_ref[...]
    ti, gt = _top4(logits)
    ti_ref[...] = ti
    gt_ref[...] = gt


def _post(yf, yb, bonus, g, pc, h, w_out_b, lnx_g, lnx_b, conv_w, out_scale, ln1_g, ln1_b,
          w_router, b_router, tm):
    t = h.shape[0]
    nblk = t // tm
    main, prev, nxt = _halo_specs(tm, CONV_COLS, nblk)

    def full(shape):
        return pl.BlockSpec(shape, lambda i: (0,) * len(shape))

    row = lambda c: pl.BlockSpec((tm, c), lambda i: (i, 0))
    srow = pl.BlockSpec((tm * SUBLANES, LANES), lambda i: (i, 0))
    wr = jnp.zeros((D_MODEL, LANES), F32).at[:, :N_EXP].set(w_router)
    br = jnp.full((1, LANES), -1e30, F32).at[0, :N_EXP].set(b_router)
    return pl.pallas_call(
        _post_kernel,
        grid=(nblk,),
        in_specs=[srow, srow, row(RW), row(RW),
                  main, prev, nxt, row(D_MODEL), full((D_MODEL, D_MODEL)),
                  full((1, RW)), full((1, RW)), full((3, CONV)), full((1, D_MODEL)),
                  full((1, D_MODEL)), full((1, D_MODEL)), full((D_MODEL, LANES)), full((1, LANES))],
        out_specs=[row(D_MODEL), row(LANES), row(LANES)],
        out_shape=[jax.ShapeDtypeStruct((t, D_MODEL), F32),
                   jax.ShapeDtypeStruct((t, LANES), jnp.int32), jax.ShapeDtypeStruct((t, LANES), F32)],
        scratch_shapes=[pltpu.VMEM((tm, D_MODEL), BF16)],
        compiler_params=_cparams(("parallel",)),
        name="mixer_out",
    )(yf, yb, bonus, g, pc, pc, pc, h, w_out_b, lnx_g.reshape(1, RW), lnx_b.reshape(1, RW), conv_w,
      out_scale.reshape(1, D_MODEL), ln1_g.reshape(1, D_MODEL), ln1_b.reshape(1, D_MODEL), wr, br)


def _moe_kernel(ve_ref, vr_ref, vn_ref, tail_ref, tok_hbm, x_hbm, wg_ref, wl_ref, bg_ref, bl_ref,
                wo_ref, bo_ref, y_hbm, xbuf, acc, wg_b, wl_b, wo_b, tok_smem, sem_tok, sem_in, sem_out):
    v = pl.program_id(0)
    f = pl.program_id(1)
    n128 = vn_ref[v]
    row0 = pl.multiple_of(vr_ref[v], MOE_BLOCK)

    def in_copy(j):
        tok = tok_smem[j >> 7, j & (MOE_BLOCK - 1)]
        return pltpu.make_async_copy(x_hbm.at[pl.ds(tok, 1), :], xbuf.at[pl.ds(j, 1), :], sem_in)

    def for_rows(n_groups, fn):
        def group(gi, c):
            for u in range(SUBLANES):
                fn(gi * SUBLANES + u)
            return c
        lax.fori_loop(0, n_groups, group, 0)

    def out_copy(j):
        r = pl.multiple_of(j * MOE_BLOCK, MOE_BLOCK)
        return pltpu.make_async_copy(acc.at[pl.ds(r, MOE_BLOCK), :],
                                     y_hbm.at[pl.ds(row0 + r, MOE_BLOCK), :], sem_out)

    @pl.when(jnp.logical_and(f == 0, n128 > 0))
    def _():
        tok_copy = pltpu.make_async_copy(
            tok_hbm.at[pl.ds(row0 // MOE_BLOCK, MOE_TMX // MOE_BLOCK), :], tok_smem, sem_tok)
        tok_copy.start()

        def init(j, c):
            rs = pl.ds(pl.multiple_of(j * MOE_BLOCK, MOE_BLOCK), MOE_BLOCK)
            acc[rs, :] = jnp.broadcast_to(bo_ref[...], (MOE_BLOCK, D_MODEL))
            return c

        lax.fori_loop(0, n128, init, 0)
        tok_copy.wait()
        n_groups = n128 * (MOE_BLOCK // SUBLANES)
        for_rows(n_groups, lambda j: in_copy(j).start())
        for_rows(n_groups, lambda j: in_copy(j).wait())

    @pl.when(n128 > 0)
    def _():
        wg_b[...] = wg_ref[...].astype(BF16)
        wl_b[...] = wl_ref[...].astype(BF16)
        wo_b[...] = wo_ref[...].astype(BF16)

        def ffn(r0, m):
            rs = pl.ds(pl.multiple_of(r0, MOE_BLOCK), m)
            x = xbuf[rs, :].astype(BF16)
            hg = jnp.dot(x, wg_b[...], preferred_element_type=F32) + bg_ref[...]
            hl = jnp.dot(x, wl_b[...], preferred_element_type=F32) + bl_ref[...]
            glu = jnp.minimum(hg, SWIGLU_LIMIT)
            lin = jnp.clip(hl, -SWIGLU_LIMIT, SWIGLU_LIMIT)
            act = glu * jax.nn.sigmoid(SWIGLU_ALPHA * glu) * (lin + 1.0)
            acc[rs, :] += jnp.dot(act.astype(BF16), wo_b[...], preferred_element_type=F32)

        def pair(i, c):
            ffn(i * (2 * MOE_SUB), MOE_SUB)
            ffn(i * (2 * MOE_SUB) + MOE_SUB, MOE_SUB)
            return c

        n_pairs = n128 >> 2
        lax.fori_loop(0, n_pairs, pair, 0)
        rem = n128 & 3
        base = n_pairs * (2 * MOE_SUB)

        @pl.when(rem >= 2)
        def _():
            ffn(base, MOE_SUB)

        @pl.when((rem & 1) == 1)
        def _():
            ffn(base + (rem >> 1) * MOE_SUB, MOE_BLOCK)

    @pl.when(jnp.logical_and(f == pl.num_programs(1) - 1, n128 > 0))
    def _():
        lax.fori_loop(0, n128, lambda j, c: (out_copy(j).start(), c)[1], 0)
        lax.fori_loop(0, n128, lambda j, c: (out_copy(j).wait(), c)[1], 0)

    @pl.when(jnp.logical_and(v == pl.num_programs(0) - 1, f == pl.num_programs(1) - 1))
    def _():
        acc[0:MOE_BLOCK, :] = jnp.zeros((MOE_BLOCK, D_MODEL), F32)

        def tail_copy(j):
            r = pl.multiple_of(tail_ref[0] + j * MOE_BLOCK, MOE_BLOCK)
            return pltpu.make_async_copy(acc.at[0:MOE_BLOCK, :], y_hbm.at[pl.ds(r, MOE_BLOCK), :], sem_out)

        lax.fori_loop(0, tail_ref[1], lambda j, c: (tail_copy(j).start(), c)[1], 0)
        lax.fori_loop(0, tail_ref[1], lambda j, c: (tail_copy(j).wait(), c)[1], 0)


def _moe_experts(h, vis_tok, vis_e, vis_row, vis_n, tail, layer, w_in, b_in, w_out, b_out, p_rows):
    nv = vis_e.shape[0]
    nf = D_EXP // MOE_TF

    def tile(v, f, vn):
        return jnp.where(vn[v] > 0, f, nf - 1)

    grid_spec = pltpu.PrefetchScalarGridSpec(
        num_scalar_prefetch=4,
        grid=(nv, nf),
        in_specs=[
            pl.BlockSpec(memory_space=pl.ANY),
            pl.BlockSpec(memory_space=pl.ANY),
            pl.BlockSpec((None, None, D_MODEL, MOE_TF), lambda v, f, ve, vr, vn, tl: (layer, ve[v], 0, tile(v, f, vn))),
            pl.BlockSpec((None, None, D_MODEL, MOE_TF), lambda v, f, ve, vr, vn, tl: (layer, ve[v], 0, nf + tile(v, f, vn))),
            pl.BlockSpec((None, None, 1, MOE_TF), lambda v, f, ve, vr, vn, tl: (layer, ve[v], 0, tile(v, f, vn))),
            pl.BlockSpec((None, None, 1, MOE_TF), lambda v, f, ve, vr, vn, tl: (layer, ve[v], 0, nf + tile(v, f, vn))),
            pl.BlockSpec((None, None, MOE_TF, D_MODEL), lambda v, f, ve, vr, vn, tl: (layer, ve[v], tile(v, f, vn), 0)),
            pl.BlockSpec((None, None, 1, D_MODEL), lambda v, f, ve, vr, vn, tl: (layer, ve[v], 0, 0)),
        ],
        out_specs=pl.BlockSpec(memory_space=pl.ANY),
        scratch_shapes=[pltpu.VMEM((MOE_TMX, D_MODEL), F32), pltpu.VMEM((MOE_TMX, D_MODEL), F32),
                        pltpu.VMEM((D_MODEL, MOE_TF), BF16), pltpu.VMEM((D_MODEL, MOE_TF), BF16),
                        pltpu.VMEM((MOE_TF, D_MODEL), BF16),
                        pltpu.SMEM((MOE_TMX // MOE_BLOCK, MOE_BLOCK), jnp.int32),
                        pltpu.SemaphoreType.DMA(()), pltpu.SemaphoreType.DMA(()),
                        pltpu.SemaphoreType.DMA(())],
    )
    return pl.pallas_call(
        _moe_kernel,
        grid_spec=grid_spec,
        out_shape=jax.ShapeDtypeStruct((p_rows, D_MODEL), F32),
        compiler_params=_cparams(("arbitrary", "arbitrary")),
        name="moe_experts",
    )(vis_e, vis_row, vis_n, tail, vis_tok, h, w_in, w_in, b_in.reshape(DEPTH, N_EXP, 1, 2 * D_EXP),
      b_in.reshape(DEPTH, N_EXP, 1, 2 * D_EXP), w_out, b_out.reshape(DEPTH, N_EXP, 1, D_MODEL))


def _route(top_i, t):
    a = t * TOP_K
    flat_e = top_i.reshape(-1)
    onehot = (flat_e[:, None] == jnp.arange(N_EXP, dtype=jnp.int32)[None, :]).astype(jnp.int32)
    csum = jnp.cumsum(onehot, axis=0)
    rank = jnp.take_along_axis(csum, flat_e[:, None], axis=1)[:, 0] - 1
    counts = csum[-1]
    padded = (counts + MOE_BLOCK - 1) // MOE_BLOCK * MOE_BLOCK
    pends = jnp.cumsum(padded)
    pstarts = pends - padded
    dest = pstarts[flat_e] + rank

    p_rows = (a + N_EXP * (MOE_BLOCK - 1) + MOE_BLOCK - 1) // MOE_BLOCK * MOE_BLOCK
    p_alloc = p_rows + MOE_TMX
    row_tok = jnp.zeros((p_alloc,), jnp.int32).at[dest].set(jnp.arange(a, dtype=jnp.int32) // TOP_K)

    n_vis = N_EXP + (p_rows + MOE_TMX - 1) // MOE_TMX
    per_e = (padded + MOE_TMX - 1) // MOE_TMX
    vend = jnp.cumsum(per_e)
    vstart = vend - per_e
    vid = jnp.arange(n_vis, dtype=jnp.int32)
    valid = vid < vend[-1]
    e_of = jnp.clip(jnp.searchsorted(vend, vid, side="right"), 0, N_EXP - 1).astype(jnp.int32)
    e_last = jnp.clip(jnp.searchsorted(vend, vend[-1] - 1, side="right"), 0, N_EXP - 1).astype(jnp.int32)
    e_of = jnp.where(valid, e_of, e_last)
    local = vid - vstart[e_of]
    vis_row = jnp.where(valid, pstarts[e_of] + local * MOE_TMX, 0).astype(jnp.int32)
    vis_rows = jnp.where(valid, jnp.clip(padded[e_of] - local * MOE_TMX, 0, MOE_TMX), 0)
    vis_n = (vis_rows // MOE_BLOCK).astype(jnp.int32)
    tail = jnp.stack([pends[-1], (p_rows - pends[-1]) // MOE_BLOCK]).astype(jnp.int32)
    return dest, row_tok.reshape(p_alloc // MOE_BLOCK, MOE_BLOCK), e_of, vis_row, vis_n, tail, p_rows


def _combine_kernel(pos_ref, y_hbm, gt_ref, h_ref, g_ref, b_ref, o_ref, ob_ref, buf, sem):
    tm = h_ref.shape[0]

    def copy(t, k):
        return pltpu.make_async_copy(y_hbm.at[pl.ds(pos_ref[0, t * TOP_K + k], 1), :],
                                     buf.at[k, pl.ds(t, 1), :], sem)

    def for_tokens(fn):
        def group(gi, c):
            for u in range(SUBLANES):
                for k in range(TOP_K):
                    fn(gi * SUBLANES + u, k)
            return c
        lax.fori_loop(0, tm // SUBLANES, group, 0)

    for_tokens(lambda t, k: copy(t, k).start())
    for_tokens(lambda t, k: copy(t, k).wait())
    gt = gt_ref[...]
    f = (gt[:, 0:1] * buf[0] + gt[:, 1:2] * buf[1] + gt[:, 2:3] * buf[2] + gt[:, 3:4] * buf[3])
    h2 = _layer_norm(ALPHA_RES * h_ref[...] + f, g_ref[...], b_ref[...])
    o_ref[...] = h2
    ob_ref[...] = h2.astype(BF16)


def _combine(yb, pos, gates, h, g, b, tm):
    t, d = h.shape
    nt = t // tm
    row = pl.BlockSpec((tm, d), lambda i: (i, 0))
    vec = pl.BlockSpec((1, d), lambda i: (0, 0))
    return pl.pallas_call(
        _combine_kernel,
        grid=(nt,),
        in_specs=[pl.BlockSpec((None, 1, tm * TOP_K), lambda i: (i, 0, 0), memory_space=pltpu.SMEM),
                  pl.BlockSpec(memory_space=pl.ANY),
                  pl.BlockSpec((tm, LANES), lambda i: (i, 0)), row, vec, vec],
        out_specs=[row, row],
        out_shape=[jax.ShapeDtypeStruct((t, d), F32), jax.ShapeDtypeStruct((t, d), BF16)],
        scratch_shapes=[pltpu.VMEM((TOP_K, tm, d), F32), pltpu.SemaphoreType.DMA(())],
        compiler_params=_cparams(("arbitrary",)),
        name="moe_combine",
    )(pos.reshape(nt, 1, tm * TOP_K), yb, gates, h, g.reshape(1, d), b.reshape(1, d))


def kernel(x, meta_tokens, ln_in_g, ln_in_b, w_in, mu_shift, decay0, decay_up, iclr0, iclr_up,
           gate_up, k_k, k_a, r_k, lnx_g, lnx_b, conv_w, out_scale, w_out, ln1_g, ln1_b,
           w_router, b_router, w_exp_in, b_exp_in, w_exp_out, b_exp_out, ln2_g, ln2_b):
    bsz, seq, d = x.shape
    assert bsz == 1 and d == D_MODEL
    t = N_META + seq
    tm = _largest_divisor(t, 16, 512)
    tm_post = _largest_divisor(t, 16, 320)
    tb = _largest_divisor(t, CHUNK, 160)

    h0 = jnp.concatenate([meta_tokens.astype(x.dtype), x[0]], axis=0)
    h, hb = _ln_in(h0, ln_in_g, ln_in_b, tm)
    for l in range(DEPTH):
        w_in_b = w_in[l].astype(BF16)
        pr = _matmul(hb, w_in_b[:, :RW_COLS], tm, RW_COLS // 3)
        pc = _matmul(hb, w_in_b[:, RW_COLS:], tm, CONV_COLS // 3)
        r, v, a, w, iw, kd, bd, bonus, g = _prep(pr, mu_shift[l], decay0[l], decay_up[l], iclr0[l],
                                                 iclr_up[l], gate_up[l], k_k[l], k_a[l], r_k[l], tb)
        yf, yb_ = _scan(r, v, a, w, iw, kd, bd, tb)
        h1, top_i, gates = _post(yf, yb_, bonus, g, pc, h, w_out[l].astype(BF16), lnx_g[l], lnx_b[l],
                                      conv_w[l], out_scale[l], ln1_g[l], ln1_b[l],
                                      w_router[l], b_router[l], tm_post)
        pos, vis_tok, vis_e, vis_row, vis_n, tail, p_rows = _route(top_i[:, :TOP_K], t)
        yb = _moe_experts(h1, vis_tok, vis_e, vis_row, vis_n, tail, l, w_exp_in, b_exp_in, w_exp_out,
                          b_exp_out, p_rows)
        h, hb = _combine(yb, pos, gates, h1, ln2_g[l], ln2_b[l], tm_post)
    return h[N_META:].reshape(bsz, seq, d)
```

```python
import functools

import jax
import jax.numpy as jnp
from jax import lax
from jax.experimental import pallas as pl
from jax.experimental.pallas import tpu as pltpu

F32 = jnp.float32
BF16 = jnp.bfloat16

D_MODEL = 2048
N_META = 16
HEAD = 64
RW = 1024
CONV = 1024
LORA = 128
RW_COLS = 3 * RW + 3 * LORA
CONV_COLS = 3 * CONV
N_EXP = 32
TOP_K = 4
D_EXP = 2048
MOE_BLOCK = 128
SWIGLU_LIMIT = 7.0
SWIGLU_ALPHA = 1.702
DEPTH = 2
ALPHA_RES = (2 * DEPTH) ** 0.25
LN_EPS = 1e-5
GN_EPS = 64e-5

LANES = 128
SUBLANES = 8
VMEM_LIMIT = 56 * 1024 * 1024

CHUNK = 16
PAIRS = RW // LANES
MOE_TMX = 1536
MOE_SUB = 256
MOE_TF = 256


def _largest_divisor(n, multiple, cap):
    best = None
    for d in range(multiple, cap + 1, multiple):
        if n % d == 0:
            best = d
    assert best is not None, (n, multiple, cap)
    return best


def _cparams(sem):
    return pltpu.CompilerParams(dimension_semantics=sem, vmem_limit_bytes=VMEM_LIMIT)


def _layer_norm(x, g, b):
    mu = jnp.mean(x, axis=-1, keepdims=True)
    xc = x - mu
    var = jnp.mean(xc * xc, axis=-1, keepdims=True)
    return xc * lax.rsqrt(var + LN_EPS) * g + b


def _head_ones2():
    r = (lax.broadcasted_iota(jnp.int32, (2 * LANES, LANES), 0) >> 6) & 1
    c = lax.broadcasted_iota(jnp.int32, (2 * LANES, LANES), 1) >> 6
    return (r == c).astype(BF16)


def _split_bf16(x):
    hi = x.astype(BF16)
    return hi, (x - hi.astype(F32)).astype(BF16)


def _head_sum(x, ones2):
    hi, lo = _split_bf16(x)
    return jnp.dot(jnp.concatenate([hi, lo], axis=1), ones2, preferred_element_type=F32)


def _ln_in_kernel(x_ref, g_ref, b_ref, h_ref, hb_ref):
    y = _layer_norm(x_ref[...], g_ref[...], b_ref[...])
    h_ref[...] = y
    hb_ref[...] = y.astype(BF16)


def _ln_in(x, g, b, tm):
    t, d = x.shape
    row = pl.BlockSpec((tm, d), lambda i: (i, 0))
    vec = pl.BlockSpec((1, d), lambda i: (0, 0))
    return pl.pallas_call(
        _ln_in_kernel,
        grid=(t // tm,),
        in_specs=[row, vec, vec],
        out_specs=[row, row],
        out_shape=[jax.ShapeDtypeStruct((t, d), F32), jax.ShapeDtypeStruct((t, d), BF16)],
        compiler_params=_cparams(("parallel",)),
        name="ln_in",
    )(x, g.reshape(1, d), b.reshape(1, d))


def _mm_kernel(a_ref, w_ref, o_ref):
    o_ref[...] = jnp.dot(a_ref[...], w_ref[...], preferred_element_type=F32)


def _matmul(a, w, tm, tn):
    t, k = a.shape
    n = w.shape[1]
    return pl.pallas_call(
        _mm_kernel,
        grid=(n // tn, t // tm),
        in_specs=[pl.BlockSpec((tm, k), lambda j, i: (i, 0)),
                  pl.BlockSpec((k, tn), lambda j, i: (0, j))],
        out_specs=pl.BlockSpec((tm, tn), lambda j, i: (i, j)),
        out_shape=jax.ShapeDtypeStruct((t, n), F32),
        compiler_params=_cparams(("parallel", "parallel")),
        name="in_proj",
    )(a, w)


def _shift_rows(x, prev_row, next_row):
    tm = x.shape[0]
    rows = lax.broadcasted_iota(jnp.int32, x.shape, 0)
    xp = jnp.where(rows == 0, prev_row, pltpu.roll(x, 1, axis=0))
    xn = jnp.where(rows == tm - 1, next_row, pltpu.roll(x, tm - 1, axis=0))
    return xp, xn


def _halo_specs(tm, cols, nblk):
    per = tm // SUBLANES
    main = pl.BlockSpec((tm, cols), lambda i: (i, 0))
    prev = pl.BlockSpec((SUBLANES, cols), lambda i: (jnp.maximum(i * per - 1, 0), 0))
    nxt = pl.BlockSpec((SUBLANES, cols), lambda i: (jnp.minimum((i + 1) * per, nblk * per - 1), 0))
    return main, prev, nxt


def _prep_kernel(p_ref, pp_ref, pn_ref, mu_ref, d0_ref, du_ref, i0_ref, iu_ref, gu_ref,
                 kk_ref, ka_ref, rk_ref,
                 r_ref, v_ref, a_ref, w_ref, iw_ref, kd_ref, bd_ref, bonus_ref, g_ref):
    i = pl.program_id(0)
    first = i == 0
    last = i == pl.num_programs(0) - 1
    ones = _head_ones2()
    tm = p_ref.shape[0]

    def mixed(c0):
        cs = slice(c0, c0 + LANES)
        x = p_ref[:, cs]
        prev_row = jnp.where(first, 0.0, pp_ref[SUBLANES - 1:SUBLANES, cs])
        next_row = jnp.where(last, 0.0, pn_ref[0:1, cs])
        xp, xn = _shift_rows(x, prev_row, next_row)
        return x + (0.5 * (xp + xn) - x) * mu_ref[:, cs]

    tanh_cw = jnp.tanh(mixed(3 * RW)).astype(BF16)
    ca = mixed(3 * RW + LORA).astype(BF16)
    sig_cg = jax.nn.sigmoid(mixed(3 * RW + 2 * LORA)).astype(BF16)

    for s in range(RW // LANES):
        cs = slice(s * LANES, (s + 1) * LANES)
        r = mixed(s * LANES)
        k = mixed(RW + s * LANES)
        v = mixed(2 * RW + s * LANES)
        kk = k * kk_ref[:, cs]
        norm = jnp.sqrt(_head_sum(kk * kk, ones))
        kk = kk / jnp.maximum(norm, 1e-12)
        pair = pl.ds(s, tm, stride=SUBLANES)
        r_ref[pair, :] = r
        v_ref[pair, :] = v
        a_ref[pair, :] = -kk
        bonus_ref[:, cs] = _head_sum(r * k * rk_ref[:, cs], ones) * v
        g_ref[:, cs] = jnp.dot(sig_cg, gu_ref[:, cs], preferred_element_type=F32)
        for d in range(2):
            ds_ = slice(d * RW + s * LANES, d * RW + (s + 1) * LANES)
            z = d0_ref[:, ds_] + jnp.dot(tanh_cw, du_ref[:, ds_], preferred_element_type=F32)
            w_log = -(jnp.maximum(-z, 0.0) + jnp.log(1.0 + jnp.exp(-jnp.abs(z)))) - 0.5
            e1 = jnp.exp(w_log)
            w_ref[d, pair, :] = jnp.exp(-e1)
            iw_ref[d, pair, :] = jnp.exp(e1)
            a_icl = jax.nn.sigmoid(i0_ref[:, ds_] + jnp.dot(ca, iu_ref[:, ds_],
                                                            preferred_element_type=F32))
            kd_ref[d, pair, :] = k * (1.0 + (a_icl - 1.0) * ka_ref[:, cs])
            bd_ref[d, pair, :] = kk * a_icl


def _block_diag2(m):
    z = jnp.zeros_like(m[0])
    return jnp.concatenate([jnp.concatenate([m[0], z], axis=1),
                            jnp.concatenate([z, m[1]], axis=1)], axis=0)


def _prep(pr, mu, decay0, decay_up, iclr0, iclr_up, gate_up, k_k, k_a, r_k, tm):
    t = pr.shape[0]
    nblk = t // tm
    main, prev, nxt = _halo_specs(tm, RW_COLS, nblk)

    def full(shape):
        return pl.BlockSpec(shape, lambda i: (0,) * len(shape))

    row = pl.BlockSpec((tm, RW), lambda i: (i, 0))
    srow = pl.BlockSpec((tm * SUBLANES, LANES), lambda i: (i, 0))
    srow2 = pl.BlockSpec((2, tm * SUBLANES, LANES), lambda i: (0, i, 0))
    one = jax.ShapeDtypeStruct((t, RW), F32)
    sone = jax.ShapeDtypeStruct((t * SUBLANES, LANES), F32)
    stwo = jax.ShapeDtypeStruct((2, t * SUBLANES, LANES), F32)
    return pl.pallas_call(
        _prep_kernel,
        grid=(nblk,),
        in_specs=[main, prev, nxt, full((1, RW_COLS)), full((1, 2 * RW)), full((LORA, 2 * RW)),
                  full((1, 2 * RW)), full((LORA, 2 * RW)), full((LORA, RW)),
                  full((1, RW)), full((1, RW)), full((1, RW))],
        out_specs=[srow, srow, srow, srow2, srow2, srow2, srow2, row, row],
        out_shape=[sone, sone, sone, stwo, stwo, stwo, stwo, one, one],
        compiler_params=_cparams(("parallel",)),
        name="rwkv_prep",
    )(pr, pr, pr, mu.reshape(1, RW_COLS), decay0.reshape(1, 2 * RW), _block_diag2(decay_up).astype(BF16),
      iclr0.reshape(1, 2 * RW), _block_diag2(iclr_up).astype(BF16), gate_up.astype(BF16),
      k_k.reshape(1, RW), k_a.reshape(1, RW), r_k.reshape(1, RW))


def _scan_products():
    prods = []
    for t in range(CHUNK):
        for s in range(t):
            prods.append(("ab", t, s))
    for t in range(CHUNK):
        for s in range(t):
            prods.append(("ak", t, s))
    for t in range(CHUNK):
        for s in range(t + 1):
            prods.append(("rb", t, s))
    for t in range(CHUNK):
        for s in range(t + 1):
            prods.append(("rk", t, s))
    return prods


_PRODS = _scan_products()
_PROD_POS = {p: (16 * (n // 4) + SUBLANES * ((n % 4) // 2), LANES * (n % 2)) for n, p in enumerate(_PRODS)}
_N_PROD_ROWS = SUBLANES * len(_PRODS) // 2
_PROD_TILE = 1024
_NT = (((1,), (1,)), ((), ()))
_TN = (((0,), (0,)), ((), ()))


class _ChunkIndex:
    def __init__(self, c, n_chunks, rev):
        c16 = CHUNK * SUBLANES
        self.rev = rev
        self.base = pl.multiple_of(((n_chunks - 1 - c) if rev else c) * c16, c16)

    def loc(self, tau):
        return ((CHUNK - 1 - tau) if self.rev else tau) * SUBLANES

    def rows(self, ref, tau):
        return ref[pl.ds(pl.multiple_of(self.base + self.loc(tau), SUBLANES), SUBLANES), :]

    def local(self, ref, tau):
        return ref[self.loc(tau):self.loc(tau) + SUBLANES, :]


def _pair_rows(ref, p):
    return ref[pl.ds(p, CHUNK, stride=SUBLANES), :]


def _scan_apply_state(sb_ref, at_ref, rt_ref, sa_ref, sr_ref):
    for p in range(PAIRS):
        ar = jnp.concatenate([_pair_rows(at_ref, p), _pair_rows(rt_ref, p)], axis=0)
        res = lax.dot_general(ar.astype(BF16), sb_ref[p], _NT, preferred_element_type=F32)
        sa_ref[pl.ds(p, CHUNK, stride=SUBLANES), :] = res[:CHUNK]
        sr_ref[pl.ds(p, CHUNK, stride=SUBLANES), :] = res[CHUNK:]


def _scan_prepare(ix, r_ref, a_ref, w_ref, iw_ref, k_ref, b_ref,
                  at_ref, rt_ref, bt_ref, kt_ref, gc_ref, lhs_ref, m_ref, ones_bd, between):
    g = None
    ig = None
    for tau in range(CHUNK):
        sl = slice(ix.loc(tau), ix.loc(tau) + SUBLANES)
        a = ix.rows(a_ref, tau)
        at_ref[sl, :] = a if g is None else a * g
        w = ix.rows(w_ref, tau)
        iw = ix.rows(iw_ref, tau)
        g = w if g is None else g * w
        ig = iw if ig is None else ig * iw
        rt_ref[sl, :] = ix.rows(r_ref, tau) * g
        bt_ref[sl, :] = ix.rows(b_ref, tau) * ig
        kt_ref[sl, :] = ix.rows(k_ref, tau) * ig
    gc_ref[...] = g
    between()

    srcs = {"a": at_ref, "r": rt_ref, "b": bt_ref, "k": kt_ref}

    def prod(n):
        kind, t, s = _PRODS[n]
        return ix.local(srcs[kind[0]], t) * ix.local(srcs[kind[1]], s)

    for q in range(len(_PRODS) // 4):
        left = jnp.concatenate([prod(4 * q), prod(4 * q + 2)], axis=0)
        right = jnp.concatenate([prod(4 * q + 1), prod(4 * q + 3)], axis=0)
        lhs_ref[16 * q:16 * (q + 1), 0:LANES] = left.astype(BF16)
        lhs_ref[16 * q:16 * (q + 1), LANES:2 * LANES] = right.astype(BF16)
    for q in range(_N_PROD_ROWS // _PROD_TILE):
        sl = slice(q * _PROD_TILE, (q + 1) * _PROD_TILE)
        m_ref[sl, :] = jnp.dot(lhs_ref[sl, :], ones_bd, preferred_element_type=F32)


def _scan_advance(ix, v_ref, y_ref, s_ref, sb_ref, bt_ref, kt_ref, gc_ref, m_ref,
                  ut_ref, sa_ref, sr_ref, diag):
    loc, rows, local = ix.loc, ix.rows, ix.local

    def coef(kind, t, s):
        r0, c0 = _PROD_POS[(kind, t, s)]
        return m_ref[r0:r0 + SUBLANES, c0:c0 + LANES]

    us = []
    for t in range(CHUNK):
        acc = local(sa_ref, t)
        for s in range(t):
            acc = acc + coef("ak", t, s) * rows(v_ref, s)
        for s in range(t):
            acc = acc + coef("ab", t, s) * us[s]
        us.append(acc)
        ut_ref[loc(t):loc(t) + SUBLANES, :] = acc

    for t in range(CHUNK):
        acc = local(sr_ref, t)
        for s in range(t + 1):
            acc = acc + coef("rk", t, s) * rows(v_ref, s) + coef("rb", t, s) * us[s]
        y_ref[pl.ds(pl.multiple_of(ix.base + loc(t), SUBLANES), SUBLANES), :] = acc

    for p in range(PAIRS):
        gp = gc_ref[p:p + 1, :]
        bk = jnp.concatenate([_pair_rows(bt_ref, p), _pair_rows(kt_ref, p)], axis=0) * gp
        uv = jnp.concatenate([_pair_rows(ut_ref, p),
                              v_ref[pl.ds(ix.base + p, CHUNK, stride=SUBLANES), :]], axis=0)
        upd = lax.dot_general(uv, bk, _TN, preferred_element_type=F32)
        s_new = s_ref[p] * gp + jnp.where(diag, upd, 0.0)
        s_ref[p] = s_new
        sb_ref[p] = s_new.astype(BF16)


_SCAN_SCRATCH_PER_DIR = 12


def _scan_kernel(*refs, n_chunks):
    ins, outs, scr = refs[:14], refs[14:16], refs[16:]
    rf, vf, af, rb, vb, ab, wf, iwf, kf, bf, wb, iwb, kb, bb = ins
    dirs = ((False, (rf, af, wf, iwf, kf, bf), vf, outs[0], scr[:_SCAN_SCRATCH_PER_DIR]),
            (True, (rb, ab, wb, iwb, kb, bb), vb, outs[1], scr[_SCAN_SCRATCH_PER_DIR:]))

    @pl.when(pl.program_id(0) == 0)
    def _():
        for _, _, _, _, st in dirs:
            for ref in st[0:2]:
                ref[...] = jnp.zeros_like(ref)

    rr = lax.broadcasted_iota(jnp.int32, (2 * LANES, 2 * LANES), 0) >> 6
    cc = lax.broadcasted_iota(jnp.int32, (2 * LANES, 2 * LANES), 1) >> 6
    ones_bd = (rr == cc).astype(BF16)
    ri = lax.broadcasted_iota(jnp.int32, (LANES, LANES), 0) >> 6
    ci = lax.broadcasted_iota(jnp.int32, (LANES, LANES), 1) >> 6
    diag = ri == ci

    def body(c, carry):
        for rev, ops, v_ref, y_ref, st in dirs:
            s, sb, at, rt, bt, kt, gc, m, lhs, ut, sa, sr = st
            ix = _ChunkIndex(c, n_chunks, rev)
            _scan_prepare(ix, *ops, at, rt, bt, kt, gc, lhs, m, ones_bd,
                          between=lambda: _scan_apply_state(sb, at, rt, sa, sr))
            _scan_advance(ix, v_ref, y_ref, s, sb, bt, kt, gc, m, ut, sa, sr, diag)
        return carry

    lax.fori_loop(0, n_chunks, body, 0)


def _scan(r, v, a, w, iw, kd, bd, tb):
    t = r.shape[0] // SUBLANES
    nb = t // tb
    rows = tb * SUBLANES
    sf = pl.BlockSpec((rows, LANES), lambda i: (i, 0))
    sb = pl.BlockSpec((rows, LANES), lambda i: (nb - 1 - i, 0))
    df = pl.BlockSpec((None, rows, LANES), lambda i: (0, i, 0))
    db = pl.BlockSpec((None, rows, LANES), lambda i: (1, nb - 1 - i, 0))
    c16 = CHUNK * SUBLANES
    per_dir = ([pltpu.VMEM((PAIRS, LANES, LANES), F32), pltpu.VMEM((PAIRS, LANES, LANES), BF16)]
               + [pltpu.VMEM((c16, LANES), F32) for _ in range(4)]
               + [pltpu.VMEM((SUBLANES, LANES), F32),
                  pltpu.VMEM((_N_PROD_ROWS, 2 * LANES), F32),
                  pltpu.VMEM((_N_PROD_ROWS, 2 * LANES), BF16)]
               + [pltpu.VMEM((c16, LANES), F32) for _ in range(3)])
    assert len(per_dir) == _SCAN_SCRATCH_PER_DIR
    out = jax.ShapeDtypeStruct((t * SUBLANES, LANES), F32)
    return pl.pallas_call(
        functools.partial(_scan_kernel, n_chunks=tb // CHUNK),
        grid=(nb,),
        in_specs=[sf, sf, sf, sb, sb, sb, df, df, df, df, db, db, db, db],
        out_specs=[sf, sb],
        out_shape=[out, out],
        scratch_shapes=per_dir + per_dir,
        compiler_params=_cparams(("arbitrary",)),
        name="rwkv_scan",
    )(r, v, a, r, v, a, w, iw, kd, bd, w, iw, kd, bd)


def _top4(logits):
    lane = lax.broadcasted_iota(jnp.int32, logits.shape, 1).astype(F32)
    vals, idxs = [], []
    l = logits
    for _ in range(TOP_K):
        m = jnp.max(l, axis=-1, keepdims=True)
        idx = jnp.min(jnp.where(l == m, lane, float(LANES)), axis=-1, keepdims=True)
        vals.append(m)
        idxs.append(idx)
        l = jnp.where(lane == idx, -jnp.inf, l)
    es = [jnp.exp(vk - vals[0]) for vk in vals]
    tot = es[0] + es[1] + es[2] + es[3]
    top_i = jnp.zeros(logits.shape, F32)
    gates = jnp.zeros(logits.shape, F32)
    for k in range(TOP_K):
        top_i = jnp.where(lane == float(k), idxs[k], top_i)
        gates = jnp.where(lane == float(k), es[k] / tot, gates)
    return top_i.astype(jnp.int32), gates


def _post_kernel(yf_ref, yb_ref, bonus_ref, g_ref, c_ref, cp_ref, cn_ref, h_ref, wo_ref,
                 lg_ref, lb_ref, cw_ref, os_ref, n1g_ref, n1b_ref, wrh_ref, wrl_ref, br_ref,
                 h1_ref, ti_ref, gt_ref, ocat_ref):
    i = pl.program_id(0)
    first = i == 0
    last = i == pl.num_programs(0) - 1
    ones = _head_ones2()
    inv_n = 1.0 / HEAD
    tm = h_ref.shape[0]

    for s in range(RW // LANES):
        cs = slice(s * LANES, (s + 1) * LANES)
        pair = pl.ds(s, tm, stride=SUBLANES)
        y = yf_ref[pair, :] + yb_ref[pair, :]
        mu = _head_sum(y, ones) * inv_n
        yc = y - mu
        var = _head_sum(yc * yc, ones) * inv_n
        yn = yc * lax.rsqrt(var + GN_EPS) * lg_ref[:, cs] + lb_ref[:, cs] + bonus_ref[:, cs]
        ocat_ref[:, cs] = (yn * g_ref[:, cs] * os_ref[:, cs]).astype(BF16)

    for s in range(CONV // LANES):
        cs = slice(s * LANES, (s + 1) * LANES)
        gs = slice(CONV + s * LANES, CONV + (s + 1) * LANES)
        hs = slice(2 * CONV + s * LANES, 2 * CONV + (s + 1) * LANES)
        u = c_ref[:, gs] * c_ref[:, hs]
        u_prev = jnp.where(first, 0.0, cp_ref[SUBLANES - 1:SUBLANES, gs] * cp_ref[SUBLANES - 1:SUBLANES, hs])
        u_next = jnp.where(last, 0.0, cn_ref[0:1, gs] * cn_ref[0:1, hs])
        up, un = _shift_rows(u, u_prev, u_next)
        conv = cw_ref[0:1, cs] * up + cw_ref[1:2, cs] * u + cw_ref[2:3, cs] * un
        oc = slice(RW + s * LANES, RW + (s + 1) * LANES)
        ocat_ref[:, oc] = (c_ref[:, cs] * conv * os_ref[:, oc]).astype(BF16)

    m = jnp.dot(ocat_ref[...], wo_ref[...], preferred_element_type=F32)
    h1 = _layer_norm(ALPHA_RES * h_ref[...] + m, n1g_ref[...], n1b_ref[...])
    h1_ref[...] = h1
    h_hi, h_lo = _split_bf16(h1)
    logits = (jnp.dot(h_hi, wrh_ref[...], preferred_element_type=F32)
              + jnp.dot(h_lo, wrh_ref[...], preferred_element_type=F32)
              + jnp.dot(h_hi, wrl_ref[...], preferred_element_type=F32)) + br_ref[...]
    ti, gt = _top4(logits)
    ti_ref[...] = ti
    gt_ref[...] = gt


def _post(yf, yb, bonus, g, pc, h, w_out_b, lnx_g, lnx_b, conv_w, out_scale, ln1_g, ln1_b,
          w_router, b_router, tm):
    t = h.shape[0]
    nblk = t // tm
    main, prev, nxt = _halo_specs(tm, CONV_COLS, nblk)

    def full(shape):
        return pl.BlockSpec(shape, lambda i: (0,) * len(shape))

    row = lambda c: pl.BlockSpec((tm, c), lambda i: (i, 0))
    srow = pl.BlockSpec((tm * SUBLANES, LANES), lambda i: (i, 0))
    wr = jnp.zeros((D_MODEL, LANES), F32).at[:, :N_EXP].set(w_router)
    wr_hi, wr_lo = _split_bf16(wr)
    br = jnp.full((1, LANES), -1e30, F32).at[0, :N_EXP].set(b_router)
    return pl.pallas_call(
        _post_kernel,
        grid=(nblk,),
        in_specs=[srow, srow, row(RW), row(RW),
                  main, prev, nxt, row(D_MODEL), full((D_MODEL, D_MODEL)),
                  full((1, RW)), full((1, RW)), full((3, CONV)), full((1, D_MODEL)),
                  full((1, D_MODEL)), full((1, D_MODEL)), full((D_MODEL, LANES)), full((D_MODEL, LANES)),
                  full((1, LANES))],
        out_specs=[row(D_MODEL), row(LANES), row(LANES)],
        out_shape=[jax.ShapeDtypeStruct((t, D_MODEL), F32),
                   jax.ShapeDtypeStruct((t, LANES), jnp.int32), jax.ShapeDtypeStruct((t, LANES), F32)],
        scratch_shapes=[pltpu.VMEM((tm, D_MODEL), BF16)],
        compiler_params=_cparams(("parallel",)),
        name="mixer_out",
    )(yf, yb, bonus, g, pc, pc, pc, h, w_out_b, lnx_g.reshape(1, RW), lnx_b.reshape(1, RW), conv_w,
      out_scale.reshape(1, D_MODEL), ln1_g.reshape(1, D_MODEL), ln1_b.reshape(1, D_MODEL), wr_hi, wr_lo, br)


def _moe_kernel(ve_ref, vr_ref, vn_ref, tail_ref, tok_hbm, x_hbm, wg_ref, wl_ref, bg_ref, bl_ref,
                wo_ref, bo_ref, y_hbm, xbuf, acc, wg_b, wl_b, wo_b, tok_smem, sem_tok, sem_in, sem_out):
    v = pl.program_id(0)
    f = pl.program_id(1)
    last_v = pl.num_programs(0) - 1
    last_f = pl.num_programs(1) - 1
    n128 = vn_ref[v]
    row0 = pl.multiple_of(vr_ref[v], MOE_BLOCK)
    nxt = jnp.minimum(v + 1, last_v)
    n128_next = jnp.where(v < last_v, vn_ref[nxt], 0)
    prv = jnp.maximum(v - 1, 0)
    n128_prev = jnp.where(v > 0, vn_ref[prv], 0)
    groups_per_block = MOE_BLOCK // SUBLANES

    def in_copy(j):
        tok = tok_smem[j >> 7, j & (MOE_BLOCK - 1)]
        return pltpu.make_async_copy(x_hbm.at[pl.ds(tok, 1), :], xbuf.at[pl.ds(j, 1), :], sem_in)

    def for_rows(n_groups, fn):
        def group(gi, c):
            for u in range(SUBLANES):
                fn(gi * SUBLANES + u)
            return c
        lax.fori_loop(0, n_groups, group, 0)

    def start_gather(first_row, nblk):
        tok_copy = pltpu.make_async_copy(
            tok_hbm.at[pl.ds(first_row // MOE_BLOCK, MOE_TMX // MOE_BLOCK), :], tok_smem, sem_tok)
        tok_copy.start()
        tok_copy.wait()
        for_rows(nblk * groups_per_block, lambda j: in_copy(j).start())

    def out_copy(j, first_row):
        r = pl.multiple_of(j * MOE_BLOCK, MOE_BLOCK)
        return pltpu.make_async_copy(acc.at[pl.ds(r, MOE_BLOCK), :],
                                     y_hbm.at[pl.ds(first_row + r, MOE_BLOCK), :], sem_out)

    @pl.when(jnp.logical_and(jnp.logical_and(v == 0, f == 0), n128 > 0))
    def _():
        start_gather(row0, n128)

    @pl.when(f == 0)
    def _():
        prev_row0 = pl.multiple_of(vr_ref[prv], MOE_BLOCK)
        lax.fori_loop(0, n128_prev, lambda j, c: (out_copy(j, prev_row0).wait(), c)[1], 0)

    @pl.when(jnp.logical_and(f == 0, n128 > 0))
    def _():
        def init(j, c):
            rs = pl.ds(pl.multiple_of(j * MOE_BLOCK, MOE_BLOCK), MOE_BLOCK)
            acc[rs, :] = jnp.broadcast_to(bo_ref[...], (MOE_BLOCK, D_MODEL))
            return c

        lax.fori_loop(0, n128, init, 0)
        for_rows(n128 * groups_per_block, lambda j: in_copy(j).wait())

    @pl.when(n128 > 0)
    def _():
        wg_b[...] = wg_ref[...].astype(BF16)
        wl_b[...] = wl_ref[...].astype(BF16)
        wo_b[...] = wo_ref[...].astype(BF16)

        def ffn(r0, m):
            rs = pl.ds(pl.multiple_of(r0, MOE_BLOCK), m)
            x = xbuf[rs, :].astype(BF16)
            hg = jnp.dot(x, wg_b[...], preferred_element_type=F32) + bg_ref[...]
            hl = jnp.dot(x, wl_b[...], preferred_element_type=F32) + bl_ref[...]
            glu = jnp.minimum(hg, SWIGLU_LIMIT)
            lin = jnp.clip(hl, -SWIGLU_LIMIT, SWIGLU_LIMIT)
            act = glu * jax.nn.sigmoid(SWIGLU_ALPHA * glu) * (lin + 1.0)
            acc[rs, :] += jnp.dot(act.astype(BF16), wo_b[...], preferred_element_type=F32)

        def pair(i, c):
            ffn(i * (2 * MOE_SUB), MOE_SUB)
            ffn(i * (2 * MOE_SUB) + MOE_SUB, MOE_SUB)
            return c

        n_pairs = n128 >> 2
        lax.fori_loop(0, n_pairs, pair, 0)
        rem = n128 & 3
        base = n_pairs * (2 * MOE_SUB)

        @pl.when(rem >= 2)
        def _():
            ffn(base, MOE_SUB)

        @pl.when((rem & 1) == 1)
        def _():
            ffn(base + (rem >> 1) * MOE_SUB, MOE_BLOCK)

    @pl.when(jnp.logical_and(f == last_f, n128 > 0))
    def _():
        lax.fori_loop(0, n128, lambda j, c: (out_copy(j, row0).start(), c)[1], 0)

    @pl.when(jnp.logical_and(f == last_f, n128_next > 0))
    def _():
        start_gather(pl.multiple_of(vr_ref[nxt], MOE_BLOCK), n128_next)

    @pl.when(jnp.logical_and(v == last_v, f == last_f))
    def _():
        lax.fori_loop(0, n128, lambda j, c: (out_copy(j, row0).wait(), c)[1], 0)
        acc[0:MOE_BLOCK, :] = jnp.zeros((MOE_BLOCK, D_MODEL), F32)

        def tail_copy(j):
            r = pl.multiple_of(tail_ref[0] + j * MOE_BLOCK, MOE_BLOCK)
            return pltpu.make_async_copy(acc.at[0:MOE_BLOCK, :], y_hbm.at[pl.ds(r, MOE_BLOCK), :], sem_out)

        lax.fori_loop(0, tail_ref[1], lambda j, c: (tail_copy(j).start(), c)[1], 0)
        lax.fori_loop(0, tail_ref[1], lambda j, c: (tail_copy(j).wait(), c)[1], 0)


def _moe_experts(h, vis_tok, vis_e, vis_row, vis_n, tail, layer, w_in, b_in, w_out, b_out, p_rows):
    nv = vis_e.shape[0]
    nf = D_EXP // MOE_TF

    def tile(v, f, vn):
        return jnp.where(vn[v] > 0, f, nf - 1)

    grid_spec = pltpu.PrefetchScalarGridSpec(
        num_scalar_prefetch=4,
        grid=(nv, nf),
        in_specs=[
            pl.BlockSpec(memory_space=pl.ANY),
            pl.BlockSpec(memory_space=pl.ANY),
            pl.BlockSpec((None, None, D_MODEL, MOE_TF), lambda v, f, ve, vr, vn, tl: (layer, ve[v], 0, tile(v, f, vn))),
            pl.BlockSpec((None, None, D_MODEL, MOE_TF), lambda v, f, ve, vr, vn, tl: (layer, ve[v], 0, nf + tile(v, f, vn))),
            pl.BlockSpec((None, None, 1, MOE_TF), lambda v, f, ve, vr, vn, tl: (layer, ve[v], 0, tile(v, f, vn))),
            pl.BlockSpec((None, None, 1, MOE_TF), lambda v, f, ve, vr, vn, tl: (layer, ve[v], 0, nf + tile(v, f, vn))),
            pl.BlockSpec((None, None, MOE_TF, D_MODEL), lambda v, f, ve, vr, vn, tl: (layer, ve[v], tile(v, f, vn), 0)),
            pl.BlockSpec((None, None, 1, D_MODEL), lambda v, f, ve, vr, vn, tl: (layer, ve[v], 0, 0)),
        ],
        out_specs=pl.BlockSpec(memory_space=pl.ANY),
        scratch_shapes=[pltpu.VMEM((MOE_TMX, D_MODEL), F32), pltpu.VMEM((MOE_TMX, D_MODEL), F32),
                        pltpu.VMEM((D_MODEL, MOE_TF), BF16), pltpu.VMEM((D_MODEL, MOE_TF), BF16),
                        pltpu.VMEM((MOE_TF, D_MODEL), BF16),
                        pltpu.SMEM((MOE_TMX // MOE_BLOCK, MOE_BLOCK), jnp.int32),
                        pltpu.SemaphoreType.DMA(()), pltpu.SemaphoreType.DMA(()),
                        pltpu.SemaphoreType.DMA(())],
    )
    return pl.pallas_call(
        _moe_kernel,
        grid_spec=grid_spec,
        out_shape=jax.ShapeDtypeStruct((p_rows, D_MODEL), F32),
        compiler_params=_cparams(("arbitrary", "arbitrary")),
        name="moe_experts",
    )(vis_e, vis_row, vis_n, tail, vis_tok, h, w_in, w_in, b_in.reshape(DEPTH, N_EXP, 1, 2 * D_EXP),
      b_in.reshape(DEPTH, N_EXP, 1, 2 * D_EXP), w_out, b_out.reshape(DEPTH, N_EXP, 1, D_MODEL))


def _route(top_i, t):
    a = t * TOP_K
    flat_e = top_i.reshape(-1)
    onehot = (flat_e[:, None] == jnp.arange(N_EXP, dtype=jnp.int32)[None, :]).astype(jnp.int32)
    csum = jnp.cumsum(onehot, axis=0)
    rank = jnp.take_along_axis(csum, flat_e[:, None], axis=1)[:, 0] - 1
    counts = csum[-1]
    padded = (counts + MOE_BLOCK - 1) // MOE_BLOCK * MOE_BLOCK
    pends = jnp.cumsum(padded)
    pstarts = pends - padded
    dest = pstarts[flat_e] + rank

    p_rows = (a + N_EXP * (MOE_BLOCK - 1) + MOE_BLOCK - 1) // MOE_BLOCK * MOE_BLOCK
    p_alloc = p_rows + MOE_TMX
    row_tok = jnp.zeros((p_alloc,), jnp.int32).at[dest].set(jnp.arange(a, dtype=jnp.int32) // TOP_K)

    n_vis = N_EXP + (p_rows + MOE_TMX - 1) // MOE_TMX
    per_e = (padded + MOE_TMX - 1) // MOE_TMX
    vend = jnp.cumsum(per_e)
    vstart = vend - per_e
    vid = jnp.arange(n_vis, dtype=jnp.int32)
    valid = vid < vend[-1]
    e_of = jnp.clip(jnp.searchsorted(vend, vid, side="right"), 0, N_EXP - 1).astype(jnp.int32)
    e_last = jnp.clip(jnp.searchsorted(vend, vend[-1] - 1, side="right"), 0, N_EXP - 1).astype(jnp.int32)
    e_of = jnp.where(valid, e_of, e_last)
    local = vid - vstart[e_of]
    vis_row = jnp.where(valid, pstarts[e_of] + local * MOE_TMX, 0).astype(jnp.int32)
    vis_rows = jnp.where(valid, jnp.clip(padded[e_of] - local * MOE_TMX, 0, MOE_TMX), 0)
    vis_n = (vis_rows // MOE_BLOCK).astype(jnp.int32)
    tail = jnp.stack([pends[-1], (p_rows - pends[-1]) // MOE_BLOCK]).astype(jnp.int32)
    return dest, row_tok.reshape(p_alloc // MOE_BLOCK, MOE_BLOCK), e_of, vis_row, vis_n, tail, p_rows


def _combine_kernel(pos_ref, y_hbm, gt_ref, h_ref, g_ref, b_ref, o_ref, ob_ref, buf, sem):
    tm = h_ref.shape[0]

    def copy(t, k):
        return pltpu.make_async_copy(y_hbm.at[pl.ds(pos_ref[0, t * TOP_K + k], 1), :],
                                     buf.at[k, pl.ds(t, 1), :], sem)

    def for_tokens(fn):
        def group(gi, c):
            for u in range(SUBLANES):
                for k in range(TOP_K):
                    fn(gi * SUBLANES + u, k)
            return c
        lax.fori_loop(0, tm // SUBLANES, group, 0)

    for_tokens(lambda t, k: copy(t, k).start())
    for_tokens(lambda t, k: copy(t, k).wait())
    gt = gt_ref[...]
    f = (gt[:, 0:1] * buf[0] + gt[:, 1:2] * buf[1] + gt[:, 2:3] * buf[2] + gt[:, 3:4] * buf[3])
    h2 = _layer_norm(ALPHA_RES * h_ref[...] + f, g_ref[...], b_ref[...])
    o_ref[...] = h2
    ob_ref[...] = h2.astype(BF16)


def _combine(yb, pos, gates, h, g, b, tm):
    t, d = h.shape
    nt = t // tm
    row = pl.BlockSpec((tm, d), lambda i: (i, 0))
    vec = pl.BlockSpec((1, d), lambda i: (0, 0))
    return pl.pallas_call(
        _combine_kernel,
        grid=(nt,),
        in_specs=[pl.BlockSpec((None, 1, tm * TOP_K), lambda i: (i, 0, 0), memory_space=pltpu.SMEM),
                  pl.BlockSpec(memory_space=pl.ANY),
                  pl.BlockSpec((tm, LANES), lambda i: (i, 0)), row, vec, vec],
        out_specs=[row, row],
        out_shape=[jax.ShapeDtypeStruct((t, d), F32), jax.ShapeDtypeStruct((t, d), BF16)],
        scratch_shapes=[pltpu.VMEM((TOP_K, tm, d), F32), pltpu.SemaphoreType.DMA(())],
        compiler_params=_cparams(("arbitrary",)),
        name="moe_combine",
    )(pos.reshape(nt, 1, tm * TOP_K), yb, gates, h, g.reshape(1, d), b.reshape(1, d))


def kernel(x, meta_tokens, ln_in_g, ln_in_b, w_in, mu_shift, decay0, decay_up, iclr0, iclr_up,
           gate_up, k_k, k_a, r_k, lnx_g, lnx_b, conv_w, out_scale, w_out, ln1_g, ln1_b,
           w_router, b_router, w_exp_in, b_exp_in, w_exp_out, b_exp_out, ln2_g, ln2_b):
    bsz, seq, d = x.shape
    assert bsz == 1 and d == D_MODEL
    t = N_META + seq
    tm = _largest_divisor(t, 16, 512)
    tm_post = _largest_divisor(t, 16, 320)
    tb = _largest_divisor(t, CHUNK, 160)

    h0 = jnp.concatenate([meta_tokens.astype(x.dtype), x[0]], axis=0)
    h, hb = _ln_in(h0, ln_in_g, ln_in_b, tm)
    for l in range(DEPTH):
        w_in_b = w_in[l].astype(BF16)
        pr = _matmul(hb, w_in_b[:, :RW_COLS], tm, RW_COLS // 3)
        pc = _matmul(hb, w_in_b[:, RW_COLS:], tm, CONV_COLS // 3)
        r, v, a, w, iw, kd, bd, bonus, g = _prep(pr, mu_shift[l], decay0[l], decay_up[l], iclr0[l],
                                                 iclr_up[l], gate_up[l], k_k[l], k_a[l], r_k[l], tb)
        yf, yb_ = _scan(r, v, a, w, iw, kd, bd, tb)
        h1, top_i, gates = _post(yf, yb_, bonus, g, pc, h, w_out[l].astype(BF16), lnx_g[l], lnx_b[l],
                                      conv_w[l], out_scale[l], ln1_g[l], ln1_b[l],
                                      w_router[l], b_router[l], tm_post)
        pos, vis_tok, vis_e, vis_row, vis_n, tail, p_rows = _route(top_i[:, :TOP_K], t)
        yb = _moe_experts(h1, vis_tok, vis_e, vis_row, vis_n, tail, l, w_exp_in, b_exp_in, w_exp_out,
                          b_exp_out, p_rows)
        h, hb = _combine(yb, pos, gates, h1, ln2_g[l], ln2_b[l], tm_post)
    return h[N_META:].reshape(bsz, seq, d)
```

```python
import functools

import jax
import jax.numpy as jnp
from jax import lax
from jax.experimental import pallas as pl
from jax.experimental.pallas import tpu as pltpu

F32 = jnp.float32
BF16 = jnp.bfloat16

D_MODEL = 2048
N_META = 16
HEAD = 64
RW = 1024
CONV = 1024
LORA = 128
RW_COLS = 3 * RW + 3 * LORA
CONV_COLS = 3 * CONV
N_EXP = 32
TOP_K = 4
D_EXP = 2048
MOE_BLOCK = 128
SWIGLU_LIMIT = 7.0
SWIGLU_ALPHA = 1.702
DEPTH = 2
ALPHA_RES = (2 * DEPTH) ** 0.25
LN_EPS = 1e-5
GN_EPS = 64e-5

LANES = 128
SUBLANES = 8
VMEM_LIMIT = 56 * 1024 * 1024

CHUNK = 16
PAIRS = RW // LANES
MOE_TMX = 1536
MOE_SUB = 256
MOE_TF = 256


def _largest_divisor(n, multiple, cap):
    best = None
    for d in range(multiple, cap + 1, multiple):
        if n % d == 0:
            best = d
    assert best is not None, (n, multiple, cap)
    return best


def _cparams(sem):
    return pltpu.CompilerParams(dimension_semantics=sem, vmem_limit_bytes=VMEM_LIMIT)


def _layer_norm(x, g, b):
    mu = jnp.mean(x, axis=-1, keepdims=True)
    xc = x - mu
    var = jnp.mean(xc * xc, axis=-1, keepdims=True)
    return xc * lax.rsqrt(var + LN_EPS) * g + b


def _head_ones2():
    r = (lax.broadcasted_iota(jnp.int32, (2 * LANES, LANES), 0) >> 6) & 1
    c = lax.broadcasted_iota(jnp.int32, (2 * LANES, LANES), 1) >> 6
    return (r == c).astype(BF16)


def _split_bf16(x):
    hi = x.astype(BF16)
    return hi, (x - hi.astype(F32)).astype(BF16)


def _head_sum(x, ones2):
    hi, lo = _split_bf16(x)
    return jnp.dot(jnp.concatenate([hi, lo], axis=1), ones2, preferred_element_type=F32)


def _ln_in_kernel(x_ref, g_ref, b_ref, h_ref, hb_ref):
    y = _layer_norm(x_ref[...], g_ref[...], b_ref[...])
    h_ref[...] = y
    hb_ref[...] = y.astype(BF16)


def _ln_in(x, g, b, tm):
    t, d = x.shape
    row = pl.BlockSpec((tm, d), lambda i: (i, 0))
    vec = pl.BlockSpec((1, d), lambda i: (0, 0))
    return pl.pallas_call(
        _ln_in_kernel,
        grid=(t // tm,),
        in_specs=[row, vec, vec],
        out_specs=[row, row],
        out_shape=[jax.ShapeDtypeStruct((t, d), F32), jax.ShapeDtypeStruct((t, d), BF16)],
        compiler_params=_cparams(("parallel",)),
        name="ln_in",
    )(x, g.reshape(1, d), b.reshape(1, d))


def _mm_kernel(a_ref, w_ref, o_ref):
    o_ref[...] = jnp.dot(a_ref[...], w_ref[...], preferred_element_type=F32)


def _matmul(a, w, tm, tn):
    t, k = a.shape
    n = w.shape[1]
    return pl.pallas_call(
        _mm_kernel,
        grid=(n // tn, t // tm),
        in_specs=[pl.BlockSpec((tm, k), lambda j, i: (i, 0)),
                  pl.BlockSpec((k, tn), lambda j, i: (0, j))],
        out_specs=pl.BlockSpec((tm, tn), lambda j, i: (i, j)),
        out_shape=jax.ShapeDtypeStruct((t, n), F32),
        compiler_params=_cparams(("parallel", "parallel")),
        name="in_proj",
    )(a, w)


def _shift_rows(x, prev_row, next_row):
    tm = x.shape[0]
    rows = lax.broadcasted_iota(jnp.int32, x.shape, 0)
    xp = jnp.where(rows == 0, prev_row, pltpu.roll(x, 1, axis=0))
    xn = jnp.where(rows == tm - 1, next_row, pltpu.roll(x, tm - 1, axis=0))
    return xp, xn


def _halo_specs(tm, cols, nblk):
    per = tm // SUBLANES
    main = pl.BlockSpec((tm, cols), lambda i: (i, 0))
    prev = pl.BlockSpec((SUBLANES, cols), lambda i: (jnp.maximum(i * per - 1, 0), 0))
    nxt = pl.BlockSpec((SUBLANES, cols), lambda i: (jnp.minimum((i + 1) * per, nblk * per - 1), 0))
    return main, prev, nxt


def _prep_kernel(p_ref, pp_ref, pn_ref, mu_ref, d0_ref, du_ref, i0_ref, iu_ref, gu_ref,
                 kk_ref, ka_ref, rk_ref,
                 r_ref, v_ref, a_ref, w_ref, iw_ref, kd_ref, bd_ref, bonus_ref, g_ref):
    i = pl.program_id(0)
    first = i == 0
    last = i == pl.num_programs(0) - 1
    ones = _head_ones2()
    tm = p_ref.shape[0]

    def mixed(c0):
        cs = slice(c0, c0 + LANES)
        x = p_ref[:, cs]
        prev_row = jnp.where(first, 0.0, pp_ref[SUBLANES - 1:SUBLANES, cs])
        next_row = jnp.where(last, 0.0, pn_ref[0:1, cs])
        xp, xn = _shift_rows(x, prev_row, next_row)
        return x + (0.5 * (xp + xn) - x) * mu_ref[:, cs]

    tanh_cw = jnp.tanh(mixed(3 * RW)).astype(BF16)
    ca = mixed(3 * RW + LORA).astype(BF16)
    sig_cg = jax.nn.sigmoid(mixed(3 * RW + 2 * LORA)).astype(BF16)

    for s in range(RW // LANES):
        cs = slice(s * LANES, (s + 1) * LANES)
        r = mixed(s * LANES)
        k = mixed(RW + s * LANES)
        v = mixed(2 * RW + s * LANES)
        kk = k * kk_ref[:, cs]
        norm = jnp.sqrt(_head_sum(kk * kk, ones))
        kk = kk / jnp.maximum(norm, 1e-12)
        pair = pl.ds(s, tm, stride=SUBLANES)
        r_ref[pair, :] = r
        v_ref[pair, :] = v
        a_ref[pair, :] = -kk
        bonus_ref[:, cs] = _head_sum(r * k * rk_ref[:, cs], ones) * v
        g_ref[:, cs] = jnp.dot(sig_cg, gu_ref[:, cs], preferred_element_type=F32)
        for d in range(2):
            ds_ = slice(d * RW + s * LANES, d * RW + (s + 1) * LANES)
            z = d0_ref[:, ds_] + jnp.dot(tanh_cw, du_ref[:, ds_], preferred_element_type=F32)
            w_log = -(jnp.maximum(-z, 0.0) + jnp.log(1.0 + jnp.exp(-jnp.abs(z)))) - 0.5
            e1 = jnp.exp(w_log)
            w_ref[d, pair, :] = jnp.exp(-e1)
            iw_ref[d, pair, :] = jnp.exp(e1)
            a_icl = jax.nn.sigmoid(i0_ref[:, ds_] + jnp.dot(ca, iu_ref[:, ds_],
                                                            preferred_element_type=F32))
            kd_ref[d, pair, :] = k * (1.0 + (a_icl - 1.0) * ka_ref[:, cs])
            bd_ref[d, pair, :] = kk * a_icl


def _block_diag2(m):
    z = jnp.zeros_like(m[0])
    return jnp.concatenate([jnp.concatenate([m[0], z], axis=1),
                            jnp.concatenate([z, m[1]], axis=1)], axis=0)


def _prep(pr, mu, decay0, decay_up, iclr0, iclr_up, gate_up, k_k, k_a, r_k, tm):
    t = pr.shape[0]
    nblk = t // tm
    main, prev, nxt = _halo_specs(tm, RW_COLS, nblk)

    def full(shape):
        return pl.BlockSpec(shape, lambda i: (0,) * len(shape))

    row = pl.BlockSpec((tm, RW), lambda i: (i, 0))
    srow = pl.BlockSpec((tm * SUBLANES, LANES), lambda i: (i, 0))
    srow2 = pl.BlockSpec((2, tm * SUBLANES, LANES), lambda i: (0, i, 0))
    one = jax.ShapeDtypeStruct((t, RW), F32)
    sone = jax.ShapeDtypeStruct((t * SUBLANES, LANES), F32)
    stwo = jax.ShapeDtypeStruct((2, t * SUBLANES, LANES), F32)
    return pl.pallas_call(
        _prep_kernel,
        grid=(nblk,),
        in_specs=[main, prev, nxt, full((1, RW_COLS)), full((1, 2 * RW)), full((LORA, 2 * RW)),
                  full((1, 2 * RW)), full((LORA, 2 * RW)), full((LORA, RW)),
                  full((1, RW)), full((1, RW)), full((1, RW))],
        out_specs=[srow, srow, srow, srow2, srow2, srow2, srow2, row, row],
        out_shape=[sone, sone, sone, stwo, stwo, stwo, stwo, one, one],
        compiler_params=_cparams(("parallel",)),
        name="rwkv_prep",
    )(pr, pr, pr, mu.reshape(1, RW_COLS), decay0.reshape(1, 2 * RW), _block_diag2(decay_up).astype(BF16),
      iclr0.reshape(1, 2 * RW), _block_diag2(iclr_up).astype(BF16), gate_up.astype(BF16),
      k_k.reshape(1, RW), k_a.reshape(1, RW), r_k.reshape(1, RW))


def _scan_products():
    prods = []
    for t in range(CHUNK):
        for s in range(t):
            prods.append(("ab", t, s))
    for t in range(CHUNK):
        for s in range(t):
            prods.append(("ak", t, s))
    for t in range(CHUNK):
        for s in range(t + 1):
            prods.append(("rb", t, s))
    for t in range(CHUNK):
        for s in range(t + 1):
            prods.append(("rk", t, s))
    return prods


_PRODS = _scan_products()
_PROD_POS = {p: (16 * (n // 4) + SUBLANES * ((n % 4) // 2), LANES * (n % 2)) for n, p in enumerate(_PRODS)}
_N_PROD_ROWS = SUBLANES * len(_PRODS) // 2
_PROD_TILE = 1024
_NT = (((1,), (1,)), ((), ()))
_TN = (((0,), (0,)), ((), ()))


class _ChunkIndex:
    def __init__(self, c, n_chunks, rev):
        c16 = CHUNK * SUBLANES
        self.rev = rev
        self.base = pl.multiple_of(((n_chunks - 1 - c) if rev else c) * c16, c16)

    def loc(self, tau):
        return ((CHUNK - 1 - tau) if self.rev else tau) * SUBLANES

    def rows(self, ref, tau):
        return ref[pl.ds(pl.multiple_of(self.base + self.loc(tau), SUBLANES), SUBLANES), :]

    def local(self, ref, tau):
        return ref[self.loc(tau):self.loc(tau) + SUBLANES, :]


def _pair_rows(ref, p):
    return ref[pl.ds(p, CHUNK, stride=SUBLANES), :]


def _scan_apply_state(sb_ref, at_ref, rt_ref, sa_ref, sr_ref):
    for p in range(PAIRS):
        ar = jnp.concatenate([_pair_rows(at_ref, p), _pair_rows(rt_ref, p)], axis=0)
        res = lax.dot_general(ar.astype(BF16), sb_ref[p], _NT, preferred_element_type=F32)
        sa_ref[pl.ds(p, CHUNK, stride=SUBLANES), :] = res[:CHUNK]
        sr_ref[pl.ds(p, CHUNK, stride=SUBLANES), :] = res[CHUNK:]


def _scan_prepare(ix, r_ref, a_ref, w_ref, iw_ref, k_ref, b_ref,
                  at_ref, rt_ref, bt_ref, kt_ref, gc_ref, lhs_ref, m_ref, ones_bd, between):
    g = None
    ig = None
    for tau in range(CHUNK):
        sl = slice(ix.loc(tau), ix.loc(tau) + SUBLANES)
        a = ix.rows(a_ref, tau)
        at_ref[sl, :] = a if g is None else a * g
        w = ix.rows(w_ref, tau)
        iw = ix.rows(iw_ref, tau)
        g = w if g is None else g * w
        ig = iw if ig is None else ig * iw
        rt_ref[sl, :] = ix.rows(r_ref, tau) * g
        bt_ref[sl, :] = ix.rows(b_ref, tau) * ig
        kt_ref[sl, :] = ix.rows(k_ref, tau) * ig
    gc_ref[...] = g
    between()

    srcs = {"a": at_ref, "r": rt_ref, "b": bt_ref, "k": kt_ref}

    def prod(n):
        kind, t, s = _PRODS[n]
        return ix.local(srcs[kind[0]], t) * ix.local(srcs[kind[1]], s)

    for q in range(len(_PRODS) // 4):
        left = jnp.concatenate([prod(4 * q), prod(4 * q + 2)], axis=0)
        right = jnp.concatenate([prod(4 * q + 1), prod(4 * q + 3)], axis=0)
        lhs_ref[16 * q:16 * (q + 1), 0:LANES] = left.astype(BF16)
        lhs_ref[16 * q:16 * (q + 1), LANES:2 * LANES] = right.astype(BF16)
    for q in range(_N_PROD_ROWS // _PROD_TILE):
        sl = slice(q * _PROD_TILE, (q + 1) * _PROD_TILE)
        m_ref[sl, :] = jnp.dot(lhs_ref[sl, :], ones_bd, preferred_element_type=F32)


def _scan_advance(ix, v_ref, y_ref, s_ref, sb_ref, bt_ref, kt_ref, gc_ref, m_ref,
                  ut_ref, sa_ref, sr_ref, diag):
    loc, rows, local = ix.loc, ix.rows, ix.local

    def coef(kind, t, s):
        r0, c0 = _PROD_POS[(kind, t, s)]
        return m_ref[r0:r0 + SUBLANES, c0:c0 + LANES]

    us = []
    for t in range(CHUNK):
        acc = local(sa_ref, t)
        for s in range(t):
            acc = acc + coef("ak", t, s) * rows(v_ref, s)
        for s in range(t):
            acc = acc + coef("ab", t, s) * us[s]
        us.append(acc)
        ut_ref[loc(t):loc(t) + SUBLANES, :] = acc

    for t in range(CHUNK):
        acc = local(sr_ref, t)
        for s in range(t + 1):
            acc = acc + coef("rk", t, s) * rows(v_ref, s) + coef("rb", t, s) * us[s]
        y_ref[pl.ds(pl.multiple_of(ix.base + loc(t), SUBLANES), SUBLANES), :] = acc

    for p in range(PAIRS):
        gp = gc_ref[p:p + 1, :]
        bk = jnp.concatenate([_pair_rows(bt_ref, p), _pair_rows(kt_ref, p)], axis=0) * gp
        uv = jnp.concatenate([_pair_rows(ut_ref, p),
                              v_ref[pl.ds(ix.base + p, CHUNK, stride=SUBLANES), :]], axis=0)
        upd = lax.dot_general(uv, bk, _TN, preferred_element_type=F32)
        s_new = s_ref[p] * gp + jnp.where(diag, upd, 0.0)
        s_ref[p] = s_new
        sb_ref[p] = s_new.astype(BF16)


_SCAN_SCRATCH_PER_DIR = 12


def _scan_kernel(*refs, n_chunks):
    ins, outs, scr = refs[:14], refs[14:16], refs[16:]
    rf, vf, af, rb, vb, ab, wf, iwf, kf, bf, wb, iwb, kb, bb = ins
    dirs = ((False, (rf, af, wf, iwf, kf, bf), vf, outs[0], scr[:_SCAN_SCRATCH_PER_DIR]),
            (True, (rb, ab, wb, iwb, kb, bb), vb, outs[1], scr[_SCAN_SCRATCH_PER_DIR:]))

    @pl.when(pl.program_id(0) == 0)
    def _():
        for _, _, _, _, st in dirs:
            for ref in st[0:2]:
                ref[...] = jnp.zeros_like(ref)

    rr = lax.broadcasted_iota(jnp.int32, (2 * LANES, 2 * LANES), 0) >> 6
    cc = lax.broadcasted_iota(jnp.int32, (2 * LANES, 2 * LANES), 1) >> 6
    ones_bd = (rr == cc).astype(BF16)
    ri = lax.broadcasted_iota(jnp.int32, (LANES, LANES), 0) >> 6
    ci = lax.broadcasted_iota(jnp.int32, (LANES, LANES), 1) >> 6
    diag = ri == ci

    def body(c, carry):
        for rev, ops, v_ref, y_ref, st in dirs:
            s, sb, at, rt, bt, kt, gc, m, lhs, ut, sa, sr = st
            ix = _ChunkIndex(c, n_chunks, rev)
            _scan_prepare(ix, *ops, at, rt, bt, kt, gc, lhs, m, ones_bd,
                          between=lambda: _scan_apply_state(sb, at, rt, sa, sr))
            _scan_advance(ix, v_ref, y_ref, s, sb, bt, kt, gc, m, ut, sa, sr, diag)
        return carry

    lax.fori_loop(0, n_chunks, body, 0)


def _scan(r, v, a, w, iw, kd, bd, tb):
    t = r.shape[0] // SUBLANES
    nb = t // tb
    rows = tb * SUBLANES
    sf = pl.BlockSpec((rows, LANES), lambda i: (i, 0))
    sb = pl.BlockSpec((rows, LANES), lambda i: (nb - 1 - i, 0))
    df = pl.BlockSpec((None, rows, LANES), lambda i: (0, i, 0))
    db = pl.BlockSpec((None, rows, LANES), lambda i: (1, nb - 1 - i, 0))
    c16 = CHUNK * SUBLANES
    per_dir = ([pltpu.VMEM((PAIRS, LANES, LANES), F32), pltpu.VMEM((PAIRS, LANES, LANES), BF16)]
               + [pltpu.VMEM((c16, LANES), F32) for _ in range(4)]
               + [pltpu.VMEM((SUBLANES, LANES), F32),
                  pltpu.VMEM((_N_PROD_ROWS, 2 * LANES), F32),
                  pltpu.VMEM((_N_PROD_ROWS, 2 * LANES), BF16)]
               + [pltpu.VMEM((c16, LANES), F32) for _ in range(3)])
    assert len(per_dir) == _SCAN_SCRATCH_PER_DIR
    out = jax.ShapeDtypeStruct((t * SUBLANES, LANES), F32)
    return pl.pallas_call(
        functools.partial(_scan_kernel, n_chunks=tb // CHUNK),
        grid=(nb,),
        in_specs=[sf, sf, sf, sb, sb, sb, df, df, df, df, db, db, db, db],
        out_specs=[sf, sb],
        out_shape=[out, out],
        scratch_shapes=per_dir + per_dir,
        compiler_params=_cparams(("arbitrary",)),
        name="rwkv_scan",
    )(r, v, a, r, v, a, w, iw, kd, bd, w, iw, kd, bd)


def _top4(logits):
    lane = lax.broadcasted_iota(jnp.int32, logits.shape, 1).astype(F32)
    vals, idxs = [], []
    l = logits
    for _ in range(TOP_K):
        m = jnp.max(l, axis=-1, keepdims=True)
        idx = jnp.min(jnp.where(l == m, lane, float(LANES)), axis=-1, keepdims=True)
        vals.append(m)
        idxs.append(idx)
        l = jnp.where(lane == idx, -jnp.inf, l)
    es = [jnp.exp(vk - vals[0]) for vk in vals]
    tot = es[0] + es[1] + es[2] + es[3]
    top_i = jnp.zeros(logits.shape, F32)
    gates = jnp.zeros(logits.shape, F32)
    for k in range(TOP_K):
        top_i = jnp.where(lane == float(k), idxs[k], top_i)
        gates = jnp.where(lane == float(k), es[k] / tot, gates)
    return top_i.astype(jnp.int32), gates


def _post_kernel(yf_ref, yb_ref, bonus_ref, g_ref, c_ref, cp_ref, cn_ref, h_ref, wo_ref,
                 lg_ref, lb_ref, cw_ref, os_ref, n1g_ref, n1b_ref, wrh_ref, wrl_ref, br_ref,
                 h1_ref, ti_ref, gt_ref, ocat_ref):
    i = pl.program_id(0)
    first = i == 0
    last = i == pl.num_programs(0) - 1
    ones = _head_ones2()
    inv_n = 1.0 / HEAD
    tm = h_ref.shape[0]

    for s in range(RW // LANES):
        cs = slice(s * LANES, (s + 1) * LANES)
        pair = pl.ds(s, tm, stride=SUBLANES)
        y = yf_ref[pair, :] + yb_ref[pair, :]
        mu = _head_sum(y, ones) * inv_n
        yc = y - mu
        var = _head_sum(yc * yc, ones) * inv_n
        yn = yc * lax.rsqrt(var + GN_EPS) * lg_ref[:, cs] + lb_ref[:, cs] + bonus_ref[:, cs]
        ocat_ref[:, cs] = (yn * g_ref[:, cs] * os_ref[:, cs]).astype(BF16)

    for s in range(CONV // LANES):
        cs = slice(s * LANES, (s + 1) * LANES)
        gs = slice(CONV + s * LANES, CONV + (s + 1) * LANES)
        hs = slice(2 * CONV + s * LANES, 2 * CONV + (s + 1) * LANES)
        u = c_ref[:, gs] * c_ref[:, hs]
        u_prev = jnp.where(first, 0.0, cp_ref[SUBLANES - 1:SUBLANES, gs] * cp_ref[SUBLANES - 1:SUBLANES, hs])
        u_next = jnp.where(last, 0.0, cn_ref[0:1, gs] * cn_ref[0:1, hs])
        up, un = _shift_rows(u, u_prev, u_next)
        conv = cw_ref[0:1, cs] * up + cw_ref[1:2, cs] * u + cw_ref[2:3, cs] * un
        oc = slice(RW + s * LANES, RW + (s + 1) * LANES)
        ocat_ref[:, oc] = (c_ref[:, cs] * conv * os_ref[:, oc]).astype(BF16)

    m = jnp.dot(ocat_ref[...], wo_ref[...], preferred_element_type=F32)
    h1 = _layer_norm(ALPHA_RES * h_ref[...] + m, n1g_ref[...], n1b_ref[...])
    h1_ref[...] = h1
    h_hi, h_lo = _split_bf16(h1)
    logits = (jnp.dot(h_hi, wrh_ref[...], preferred_element_type=F32)
              + jnp.dot(h_lo, wrh_ref[...], preferred_element_type=F32)
              + jnp.dot(h_hi, wrl_ref[...], preferred_element_type=F32)) + br_ref[...]
    ti, gt = _top4(logits)
    ti_ref[...] = ti
    gt_ref[...] = gt


def _post(yf, yb, bonus, g, pc, h, w_out_b, lnx_g, lnx_b, conv_w, out_scale, ln1_g, ln1_b,
          w_router, b_router, tm):
    t = h.shape[0]
    nblk = t // tm
    main, prev, nxt = _halo_specs(tm, CONV_COLS, nblk)

    def full(shape):
        return pl.BlockSpec(shape, lambda i: (0,) * len(shape))

    row = lambda c: pl.BlockSpec((tm, c), lambda i: (i, 0))
    srow = pl.BlockSpec((tm * SUBLANES, LANES), lambda i: (i, 0))
    wr = jnp.zeros((D_MODEL, LANES), F32).at[:, :N_EXP].set(w_router)
    wr_hi, wr_lo = _split_bf16(wr)
    br = jnp.full((1, LANES), -1e30, F32).at[0, :N_EXP].set(b_router)
    return pl.pallas_call(
        _post_kernel,
        grid=(nblk,),
        in_specs=[srow, srow, row(RW), row(RW),
                  main, prev, nxt, row(D_MODEL), full((D_MODEL, D_MODEL)),
                  full((1, RW)), full((1, RW)), full((3, CONV)), full((1, D_MODEL)),
                  full((1, D_MODEL)), full((1, D_MODEL)), full((D_MODEL, LANES)), full((D_MODEL, LANES)),
                  full((1, LANES))],
        out_specs=[row(D_MODEL), row(LANES), row(LANES)],
        out_shape=[jax.ShapeDtypeStruct((t, D_MODEL), F32),
                   jax.ShapeDtypeStruct((t, LANES), jnp.int32), jax.ShapeDtypeStruct((t, LANES), F32)],
        scratch_shapes=[pltpu.VMEM((tm, D_MODEL), BF16)],
        compiler_params=_cparams(("parallel",)),
        name="mixer_out",
    )(yf, yb, bonus, g, pc, pc, pc, h, w_out_b, lnx_g.reshape(1, RW), lnx_b.reshape(1, RW), conv_w,
      out_scale.reshape(1, D_MODEL), ln1_g.reshape(1, D_MODEL), ln1_b.reshape(1, D_MODEL), wr_hi, wr_lo, br)


def _moe_kernel(ve_ref, vr_ref, vn_ref, tail_ref, tok_hbm, x_hbm, wg_ref, wl_ref, bg_ref, bl_ref,
                wo_ref, bo_ref, y_hbm, xbuf, acc, wg_b, wl_b, wo_b, tok_smem, sem_tok, sem_in, sem_out):
    v = pl.program_id(0)
    f = pl.program_id(1)
    last_v = pl.num_programs(0) - 1
    last_f = pl.num_programs(1) - 1
    n128 = vn_ref[v]
    row0 = pl.multiple_of(vr_ref[v], MOE_BLOCK)
    nxt = jnp.minimum(v + 1, last_v)
    n128_next = jnp.where(v < last_v, vn_ref[nxt], 0)
    prv = jnp.maximum(v - 1, 0)
    n128_prev = jnp.where(v > 0, vn_ref[prv], 0)
    groups_per_block = MOE_BLOCK // SUBLANES

    def in_copy(j):
        tok = tok_smem[j >> 7, j & (MOE_BLOCK - 1)]
        return pltpu.make_async_copy(x_hbm.at[pl.ds(tok, 1), :], xbuf.at[pl.ds(j, 1), :], sem_in)

    def for_rows(n_groups, fn):
        def group(gi, c):
            for u in range(SUBLANES):
                fn(gi * SUBLANES + u, u)
            return c
        lax.fori_loop(0, n_groups, group, 0)

    def start_gather(first_row, nblk):
        tok_copy = pltpu.make_async_copy(
            tok_hbm.at[pl.ds(first_row // MOE_BLOCK, MOE_TMX // MOE_BLOCK), :], tok_smem, sem_tok)
        tok_copy.start()
        tok_copy.wait()
        for_rows(nblk * groups_per_block, lambda j, u: in_copy(j).start(priority=u % 2))

    def out_copy(j, first_row):
        r = pl.multiple_of(j * MOE_BLOCK, MOE_BLOCK)
        return pltpu.make_async_copy(acc.at[pl.ds(r, MOE_BLOCK), :],
                                     y_hbm.at[pl.ds(first_row + r, MOE_BLOCK), :], sem_out)

    @pl.when(jnp.logical_and(jnp.logical_and(v == 0, f == 0), n128 > 0))
    def _():
        start_gather(row0, n128)

    @pl.when(f == 0)
    def _():
        prev_row0 = pl.multiple_of(vr_ref[prv], MOE_BLOCK)
        lax.fori_loop(0, n128_prev, lambda j, c: (out_copy(j, prev_row0).wait(), c)[1], 0)

    @pl.when(jnp.logical_and(f == 0, n128 > 0))
    def _():
        def init(j, c):
            rs = pl.ds(pl.multiple_of(j * MOE_BLOCK, MOE_BLOCK), MOE_BLOCK)
            acc[rs, :] = jnp.broadcast_to(bo_ref[...], (MOE_BLOCK, D_MODEL))
            return c

        lax.fori_loop(0, n128, init, 0)
        for_rows(n128 * groups_per_block, lambda j, u: in_copy(j).wait())

    @pl.when(n128 > 0)
    def _():
        wg_b[...] = wg_ref[...].astype(BF16)
        wl_b[...] = wl_ref[...].astype(BF16)
        wo_b[...] = wo_ref[...].astype(BF16)

        def ffn(r0, m):
            rs = pl.ds(pl.multiple_of(r0, MOE_BLOCK), m)
            x = xbuf[rs, :].astype(BF16)
            hg = jnp.dot(x, wg_b[...], preferred_element_type=F32) + bg_ref[...]
            hl = jnp.dot(x, wl_b[...], preferred_element_type=F32) + bl_ref[...]
            glu = jnp.minimum(hg, SWIGLU_LIMIT)
            lin = jnp.clip(hl, -SWIGLU_LIMIT, SWIGLU_LIMIT)
            act = glu * jax.nn.sigmoid(SWIGLU_ALPHA * glu) * (lin + 1.0)
            acc[rs, :] += jnp.dot(act.astype(BF16), wo_b[...], preferred_element_type=F32)

        def pair(i, c):
            ffn(i * (2 * MOE_SUB), MOE_SUB)
            ffn(i * (2 * MOE_SUB) + MOE_SUB, MOE_SUB)
            return c

        n_pairs = n128 >> 2
        lax.fori_loop(0, n_pairs, pair, 0)
        rem = n128 & 3
        base = n_pairs * (2 * MOE_SUB)

        @pl.when(rem >= 2)
        def _():
            ffn(base, MOE_SUB)

        @pl.when((rem & 1) == 1)
        def _():
            ffn(base + (rem >> 1) * MOE_SUB, MOE_BLOCK)

    @pl.when(jnp.logical_and(f == last_f, n128 > 0))
    def _():
        lax.fori_loop(0, n128, lambda j, c: (out_copy(j, row0).start(), c)[1], 0)

    @pl.when(jnp.logical_and(f == last_f, n128_next > 0))
    def _():
        start_gather(pl.multiple_of(vr_ref[nxt], MOE_BLOCK), n128_next)

    @pl.when(jnp.logical_and(v == last_v, f == last_f))
    def _():
        lax.fori_loop(0, n128, lambda j, c: (out_copy(j, row0).wait(), c)[1], 0)
        acc[0:MOE_BLOCK, :] = jnp.zeros((MOE_BLOCK, D_MODEL), F32)

        def tail_copy(j):
            r = pl.multiple_of(tail_ref[0] + j * MOE_BLOCK, MOE_BLOCK)
            return pltpu.make_async_copy(acc.at[0:MOE_BLOCK, :], y_hbm.at[pl.ds(r, MOE_BLOCK), :], sem_out)

        lax.fori_loop(0, tail_ref[1], lambda j, c: (tail_copy(j).start(), c)[1], 0)
        lax.fori_loop(0, tail_ref[1], lambda j, c: (tail_copy(j).wait(), c)[1], 0)


def _moe_experts(h, vis_tok, vis_e, vis_row, vis_n, tail, layer, w_in, b_in, w_out, b_out, p_rows):
    nv = vis_e.shape[0]
    nf = D_EXP // MOE_TF

    def tile(v, f, vn):
        return jnp.where(vn[v] > 0, f, nf - 1)

    grid_spec = pltpu.PrefetchScalarGridSpec(
        num_scalar_prefetch=4,
        grid=(nv, nf),
        in_specs=[
            pl.BlockSpec(memory_space=pl.ANY),
            pl.BlockSpec(memory_space=pl.ANY),
            pl.BlockSpec((None, None, D_MODEL, MOE_TF), lambda v, f, ve, vr, vn, tl: (layer, ve[v], 0, tile(v, f, vn))),
            pl.BlockSpec((None, None, D_MODEL, MOE_TF), lambda v, f, ve, vr, vn, tl: (layer, ve[v], 0, nf + tile(v, f, vn))),
            pl.BlockSpec((None, None, 1, MOE_TF), lambda v, f, ve, vr, vn, tl: (layer, ve[v], 0, tile(v, f, vn))),
            pl.BlockSpec((None, None, 1, MOE_TF), lambda v, f, ve, vr, vn, tl: (layer, ve[v], 0, nf + tile(v, f, vn))),
            pl.BlockSpec((None, None, MOE_TF, D_MODEL), lambda v, f, ve, vr, vn, tl: (layer, ve[v], tile(v, f, vn), 0)),
            pl.BlockSpec((None, None, 1, D_MODEL), lambda v, f, ve, vr, vn, tl: (layer, ve[v], 0, 0)),
        ],
        out_specs=pl.BlockSpec(memory_space=pl.ANY),
        scratch_shapes=[pltpu.VMEM((MOE_TMX, D_MODEL), F32), pltpu.VMEM((MOE_TMX, D_MODEL), F32),
                        pltpu.VMEM((D_MODEL, MOE_TF), BF16), pltpu.VMEM((D_MODEL, MOE_TF), BF16),
                        pltpu.VMEM((MOE_TF, D_MODEL), BF16),
                        pltpu.SMEM((MOE_TMX // MOE_BLOCK, MOE_BLOCK), jnp.int32),
                        pltpu.SemaphoreType.DMA(()), pltpu.SemaphoreType.DMA(()),
                        pltpu.SemaphoreType.DMA(())],
    )
    return pl.pallas_call(
        _moe_kernel,
        grid_spec=grid_spec,
        out_shape=jax.ShapeDtypeStruct((p_rows, D_MODEL), F32),
        compiler_params=_cparams(("arbitrary", "arbitrary")),
        name="moe_experts",
    )(vis_e, vis_row, vis_n, tail, vis_tok, h, w_in, w_in, b_in.reshape(DEPTH, N_EXP, 1, 2 * D_EXP),
      b_in.reshape(DEPTH, N_EXP, 1, 2 * D_EXP), w_out, b_out.reshape(DEPTH, N_EXP, 1, D_MODEL))


def _route(top_i, t):
    a = t * TOP_K
    flat_e = top_i.reshape(-1)
    onehot = (flat_e[:, None] == jnp.arange(N_EXP, dtype=jnp.int32)[None, :]).astype(jnp.int32)
    csum = jnp.cumsum(onehot, axis=0)
    rank = jnp.take_along_axis(csum, flat_e[:, None], axis=1)[:, 0] - 1
    counts = csum[-1]
    padded = (counts + MOE_BLOCK - 1) // MOE_BLOCK * MOE_BLOCK
    pends = jnp.cumsum(padded)
    pstarts = pends - padded
    dest = pstarts[flat_e] + rank

    p_rows = (a + N_EXP * (MOE_BLOCK - 1) + MOE_BLOCK - 1) // MOE_BLOCK * MOE_BLOCK
    p_alloc = p_rows + MOE_TMX
    row_tok = jnp.zeros((p_alloc,), jnp.int32).at[dest].set(jnp.arange(a, dtype=jnp.int32) // TOP_K)

    n_vis = N_EXP + (p_rows + MOE_TMX - 1) // MOE_TMX
    per_e = (padded + MOE_TMX - 1) // MOE_TMX
    vend = jnp.cumsum(per_e)
    vstart = vend - per_e
    vid = jnp.arange(n_vis, dtype=jnp.int32)
    valid = vid < vend[-1]
    e_of = jnp.clip(jnp.searchsorted(vend, vid, side="right"), 0, N_EXP - 1).astype(jnp.int32)
    e_last = jnp.clip(jnp.searchsorted(vend, vend[-1] - 1, side="right"), 0, N_EXP - 1).astype(jnp.int32)
    e_of = jnp.where(valid, e_of, e_last)
    local = vid - vstart[e_of]
    vis_row = jnp.where(valid, pstarts[e_of] + local * MOE_TMX, 0).astype(jnp.int32)
    vis_rows = jnp.where(valid, jnp.clip(padded[e_of] - local * MOE_TMX, 0, MOE_TMX), 0)
    vis_n = (vis_rows // MOE_BLOCK).astype(jnp.int32)
    tail = jnp.stack([pends[-1], (p_rows - pends[-1]) // MOE_BLOCK]).astype(jnp.int32)
    return dest, row_tok.reshape(p_alloc // MOE_BLOCK, MOE_BLOCK), e_of, vis_row, vis_n, tail, p_rows


def _combine_kernel(pos_ref, y_hbm, gt_ref, h_ref, g_ref, b_ref, o_ref, ob_ref, buf, sem):
    tm = h_ref.shape[0]

    def copy(t, k):
        return pltpu.make_async_copy(y_hbm.at[pl.ds(pos_ref[0, t * TOP_K + k], 1), :],
                                     buf.at[k, pl.ds(t, 1), :], sem)

    def for_tokens(fn):
        def group(gi, c):
            for u in range(SUBLANES):
                for k in range(TOP_K):
                    fn(gi * SUBLANES + u, k)
            return c
        lax.fori_loop(0, tm // SUBLANES, group, 0)

    for_tokens(lambda t, k: copy(t, k).start(priority=k % 2))
    for_tokens(lambda t, k: copy(t, k).wait())
    gt = gt_ref[...]
    f = (gt[:, 0:1] * buf[0] + gt[:, 1:2] * buf[1] + gt[:, 2:3] * buf[2] + gt[:, 3:4] * buf[3])
    h2 = _layer_norm(ALPHA_RES * h_ref[...] + f, g_ref[...], b_ref[...])
    o_ref[...] = h2
    ob_ref[...] = h2.astype(BF16)


def _combine(yb, pos, gates, h, g, b, tm):
    t, d = h.shape
    nt = t // tm
    row = pl.BlockSpec((tm, d), lambda i: (i, 0))
    vec = pl.BlockSpec((1, d), lambda i: (0, 0))
    return pl.pallas_call(
        _combine_kernel,
        grid=(nt,),
        in_specs=[pl.BlockSpec((None, 1, tm * TOP_K), lambda i: (i, 0, 0), memory_space=pltpu.SMEM),
                  pl.BlockSpec(memory_space=pl.ANY),
                  pl.BlockSpec((tm, LANES), lambda i: (i, 0)), row, vec, vec],
        out_specs=[row, row],
        out_shape=[jax.ShapeDtypeStruct((t, d), F32), jax.ShapeDtypeStruct((t, d), BF16)],
        scratch_shapes=[pltpu.VMEM((TOP_K, tm, d), F32), pltpu.SemaphoreType.DMA(())],
        compiler_params=_cparams(("arbitrary",)),
        name="moe_combine",
    )(pos.reshape(nt, 1, tm * TOP_K), yb, gates, h, g.reshape(1, d), b.reshape(1, d))


def kernel(x, meta_tokens, ln_in_g, ln_in_b, w_in, mu_shift, decay0, decay_up, iclr0, iclr_up,
           gate_up, k_k, k_a, r_k, lnx_g, lnx_b, conv_w, out_scale, w_out, ln1_g, ln1_b,
           w_router, b_router, w_exp_in, b_exp_in, w_exp_out, b_exp_out, ln2_g, ln2_b):
    bsz, seq, d = x.shape
    assert bsz == 1 and d == D_MODEL
    t = N_META + seq
    tm = _largest_divisor(t, 16, 512)
    tm_post = _largest_divisor(t, 16, 320)
    tb = _largest_divisor(t, CHUNK, 160)

    h0 = jnp.concatenate([meta_tokens.astype(x.dtype), x[0]], axis=0)
    h, hb = _ln_in(h0, ln_in_g, ln_in_b, tm)
    for l in range(DEPTH):
        w_in_b = w_in[l].astype(BF16)
        pr = _matmul(hb, w_in_b[:, :RW_COLS], tm, RW_COLS // 3)
        pc = _matmul(hb, w_in_b[:, RW_COLS:], tm, CONV_COLS // 3)
        r, v, a, w, iw, kd, bd, bonus, g = _prep(pr, mu_shift[l], decay0[l], decay_up[l], iclr0[l],
                                                 iclr_up[l], gate_up[l], k_k[l], k_a[l], r_k[l], tb)
        yf, yb_ = _scan(r, v, a, w, iw, kd, bd, tb)
        h1, top_i, gates = _post(yf, yb_, bonus, g, pc, h, w_out[l].astype(BF16), lnx_g[l], lnx_b[l],
                                      conv_w[l], out_scale[l], ln1_g[l], ln1_b[l],
                                      w_router[l], b_router[l], tm_post)
        pos, vis_tok, vis_e, vis_row, vis_n, tail, p_rows = _route(top_i[:, :TOP_K], t)
        yb = _moe_experts(h1, vis_tok, vis_e, vis_row, vis_n, tail, l, w_exp_in, b_exp_in, w_exp_out,
                          b_exp_out, p_rows)
        h, hb = _combine(yb, pos, gates, h1, ln2_g[l], ln2_b[l], tm_post)
    return h[N_META:].reshape(bsz, seq, d)
```

```python
import functools

import jax
import jax.numpy as jnp
from jax import lax
from jax.experimental import pallas as pl
from jax.experimental.pallas import tpu as pltpu

F32 = jnp.float32
BF16 = jnp.bfloat16

D_MODEL = 2048
N_META = 16
HEAD = 64
RW = 1024
CONV = 1024
LORA = 128
RW_COLS = 3 * RW + 3 * LORA
CONV_COLS = 3 * CONV
N_EXP = 32
TOP_K = 4
D_EXP = 2048
MOE_BLOCK = 128
SWIGLU_LIMIT = 7.0
SWIGLU_ALPHA = 1.702
DEPTH = 2
ALPHA_RES = (2 * DEPTH) ** 0.25
LN_EPS = 1e-5
GN_EPS = 64e-5

LANES = 128
SUBLANES = 8
VMEM_LIMIT = 56 * 1024 * 1024

CHUNK = 16
PAIRS = RW // LANES
MOE_TMX = 1536
MOE_SUB = 256
MOE_TF = 256


def _largest_divisor(n, multiple, cap):
    best = None
    for d in range(multiple, cap + 1, multiple):
        if n % d == 0:
            best = d
    assert best is not None, (n, multiple, cap)
    return best


def _cparams(sem):
    return pltpu.CompilerParams(dimension_semantics=sem, vmem_limit_bytes=VMEM_LIMIT)


def _layer_norm(x, g, b):
    mu = jnp.mean(x, axis=-1, keepdims=True)
    xc = x - mu
    var = jnp.mean(xc * xc, axis=-1, keepdims=True)
    return xc * lax.rsqrt(var + LN_EPS) * g + b


def _head_ones2():
    r = (lax.broadcasted_iota(jnp.int32, (2 * LANES, LANES), 0) >> 6) & 1
    c = lax.broadcasted_iota(jnp.int32, (2 * LANES, LANES), 1) >> 6
    return (r == c).astype(BF16)


def _split_bf16(x):
    hi = x.astype(BF16)
    return hi, (x - hi.astype(F32)).astype(BF16)


def _head_sum(x, ones2):
    hi, lo = _split_bf16(x)
    return jnp.dot(jnp.concatenate([hi, lo], axis=1), ones2, preferred_element_type=F32)


def _ln_in_kernel(x_ref, g_ref, b_ref, h_ref, hb_ref):
    y = _layer_norm(x_ref[...], g_ref[...], b_ref[...])
    h_ref[...] = y
    hb_ref[...] = y.astype(BF16)


def _ln_in(x, g, b, tm):
    t, d = x.shape
    row = pl.BlockSpec((tm, d), lambda i: (i, 0))
    vec = pl.BlockSpec((1, d), lambda i: (0, 0))
    return pl.pallas_call(
        _ln_in_kernel,
        grid=(t // tm,),
        in_specs=[row, vec, vec],
        out_specs=[row, row],
        out_shape=[jax.ShapeDtypeStruct((t, d), F32), jax.ShapeDtypeStruct((t, d), BF16)],
        compiler_params=_cparams(("parallel",)),
        name="ln_in",
    )(x, g.reshape(1, d), b.reshape(1, d))


def _mm_kernel(a_ref, w_ref, o_ref):
    o_ref[...] = jnp.dot(a_ref[...], w_ref[...], preferred_element_type=F32)


def _matmul(a, w, tm, tn):
    t, k = a.shape
    n = w.shape[1]
    return pl.pallas_call(
        _mm_kernel,
        grid=(n // tn, t // tm),
        in_specs=[pl.BlockSpec((tm, k), lambda j, i: (i, 0)),
                  pl.BlockSpec((k, tn), lambda j, i: (0, j))],
        out_specs=pl.BlockSpec((tm, tn), lambda j, i: (i, j)),
        out_shape=jax.ShapeDtypeStruct((t, n), F32),
        compiler_params=_cparams(("parallel", "parallel")),
        name="in_proj",
    )(a, w)


def _shift_rows(x, prev_row, next_row):
    tm = x.shape[0]
    rows = lax.broadcasted_iota(jnp.int32, x.shape, 0)
    xp = jnp.where(rows == 0, prev_row, pltpu.roll(x, 1, axis=0))
    xn = jnp.where(rows == tm - 1, next_row, pltpu.roll(x, tm - 1, axis=0))
    return xp, xn


def _halo_specs(tm, cols, nblk):
    per = tm // SUBLANES
    main = pl.BlockSpec((tm, cols), lambda i: (i, 0))
    prev = pl.BlockSpec((SUBLANES, cols), lambda i: (jnp.maximum(i * per - 1, 0), 0))
    nxt = pl.BlockSpec((SUBLANES, cols), lambda i: (jnp.minimum((i + 1) * per, nblk * per - 1), 0))
    return main, prev, nxt


def _prep_kernel(p_ref, pp_ref, pn_ref, mu_ref, d0_ref, du_ref, i0_ref, iu_ref, gu_ref,
                 kk_ref, ka_ref, rk_ref,
                 r_ref, v_ref, a_ref, w_ref, iw_ref, kd_ref, bd_ref, bonus_ref, g_ref):
    i = pl.program_id(0)
    first = i == 0
    last = i == pl.num_programs(0) - 1
    ones = _head_ones2()
    tm = p_ref.shape[0]

    def mixed(c0):
        cs = slice(c0, c0 + LANES)
        x = p_ref[:, cs]
        prev_row = jnp.where(first, 0.0, pp_ref[SUBLANES - 1:SUBLANES, cs])
        next_row = jnp.where(last, 0.0, pn_ref[0:1, cs])
        xp, xn = _shift_rows(x, prev_row, next_row)
        return x + (0.5 * (xp + xn) - x) * mu_ref[:, cs]

    tanh_cw = jnp.tanh(mixed(3 * RW)).astype(BF16)
    ca = mixed(3 * RW + LORA).astype(BF16)
    sig_cg = jax.nn.sigmoid(mixed(3 * RW + 2 * LORA)).astype(BF16)

    for s in range(RW // LANES):
        cs = slice(s * LANES, (s + 1) * LANES)
        r = mixed(s * LANES)
        k = mixed(RW + s * LANES)
        v = mixed(2 * RW + s * LANES)
        kk = k * kk_ref[:, cs]
        norm = jnp.sqrt(_head_sum(kk * kk, ones))
        kk = kk / jnp.maximum(norm, 1e-12)
        pair = pl.ds(s, tm, stride=SUBLANES)
        r_ref[pair, :] = r
        v_ref[pair, :] = v
        a_ref[pair, :] = -kk
        bonus_ref[:, cs] = _head_sum(r * k * rk_ref[:, cs], ones) * v
        g_ref[:, cs] = jnp.dot(sig_cg, gu_ref[:, cs], preferred_element_type=F32)
        for d in range(2):
            ds_ = slice(d * RW + s * LANES, d * RW + (s + 1) * LANES)
            z = d0_ref[:, ds_] + jnp.dot(tanh_cw, du_ref[:, ds_], preferred_element_type=F32)
            w_log = -(jnp.maximum(-z, 0.0) + jnp.log(1.0 + jnp.exp(-jnp.abs(z)))) - 0.5
            e1 = jnp.exp(w_log)
            w_ref[d, pair, :] = jnp.exp(-e1)
            iw_ref[d, pair, :] = jnp.exp(e1)
            a_icl = jax.nn.sigmoid(i0_ref[:, ds_] + jnp.dot(ca, iu_ref[:, ds_],
                                                            preferred_element_type=F32))
            kd_ref[d, pair, :] = k * (1.0 + (a_icl - 1.0) * ka_ref[:, cs])
            bd_ref[d, pair, :] = kk * a_icl


def _block_diag2(m):
    z = jnp.zeros_like(m[0])
    return jnp.concatenate([jnp.concatenate([m[0], z], axis=1),
                            jnp.concatenate([z, m[1]], axis=1)], axis=0)


def _prep(pr, mu, decay0, decay_up, iclr0, iclr_up, gate_up, k_k, k_a, r_k, tm):
    t = pr.shape[0]
    nblk = t // tm
    main, prev, nxt = _halo_specs(tm, RW_COLS, nblk)

    def full(shape):
        return pl.BlockSpec(shape, lambda i: (0,) * len(shape))

    row = pl.BlockSpec((tm, RW), lambda i: (i, 0))
    srow = pl.BlockSpec((tm * SUBLANES, LANES), lambda i: (i, 0))
    srow2 = pl.BlockSpec((2, tm * SUBLANES, LANES), lambda i: (0, i, 0))
    one = jax.ShapeDtypeStruct((t, RW), F32)
    sone = jax.ShapeDtypeStruct((t * SUBLANES, LANES), F32)
    stwo = jax.ShapeDtypeStruct((2, t * SUBLANES, LANES), F32)
    return pl.pallas_call(
        _prep_kernel,
        grid=(nblk,),
        in_specs=[main, prev, nxt, full((1, RW_COLS)), full((1, 2 * RW)), full((LORA, 2 * RW)),
                  full((1, 2 * RW)), full((LORA, 2 * RW)), full((LORA, RW)),
                  full((1, RW)), full((1, RW)), full((1, RW))],
        out_specs=[srow, srow, srow, srow2, srow2, srow2, srow2, row, row],
        out_shape=[sone, sone, sone, stwo, stwo, stwo, stwo, one, one],
        compiler_params=_cparams(("parallel",)),
        name="rwkv_prep",
    )(pr, pr, pr, mu.reshape(1, RW_COLS), decay0.reshape(1, 2 * RW), _block_diag2(decay_up).astype(BF16),
      iclr0.reshape(1, 2 * RW), _block_diag2(iclr_up).astype(BF16), gate_up.astype(BF16),
      k_k.reshape(1, RW), k_a.reshape(1, RW), r_k.reshape(1, RW))


def _scan_products():
    prods = []
    for t in range(CHUNK):
        for s in range(t):
            prods.append(("ab", t, s))
    for t in range(CHUNK):
        for s in range(t):
            prods.append(("ak", t, s))
    for t in range(CHUNK):
        for s in range(t + 1):
            prods.append(("rb", t, s))
    for t in range(CHUNK):
        for s in range(t + 1):
            prods.append(("rk", t, s))
    return prods


_PRODS = _scan_products()
_PROD_POS = {p: (16 * (n // 4) + SUBLANES * ((n % 4) // 2), LANES * (n % 2)) for n, p in enumerate(_PRODS)}
_N_PROD_ROWS = SUBLANES * len(_PRODS) // 2
_PROD_TILE = 512
_NT = (((1,), (1,)), ((), ()))
_TN = (((0,), (0,)), ((), ()))


class _ChunkIndex:
    def __init__(self, c, n_chunks, rev):
        c16 = CHUNK * SUBLANES
        self.rev = rev
        self.base = pl.multiple_of(((n_chunks - 1 - c) if rev else c) * c16, c16)

    def loc(self, tau):
        return ((CHUNK - 1 - tau) if self.rev else tau) * SUBLANES

    def rows(self, ref, tau):
        return ref[pl.ds(pl.multiple_of(self.base + self.loc(tau), SUBLANES), SUBLANES), :]

    def local(self, ref, tau):
        return ref[self.loc(tau):self.loc(tau) + SUBLANES, :]


def _pair_rows(ref, p):
    return ref[pl.ds(p, CHUNK, stride=SUBLANES), :]


def _scan_apply_state(sb_ref, at_ref, rt_ref, sa_ref, sr_ref):
    for p in range(PAIRS):
        ar = jnp.concatenate([_pair_rows(at_ref, p), _pair_rows(rt_ref, p)], axis=0)
        res = lax.dot_general(ar.astype(BF16), sb_ref[p], _NT, preferred_element_type=F32)
        sa_ref[pl.ds(p, CHUNK, stride=SUBLANES), :] = res[:CHUNK]
        sr_ref[pl.ds(p, CHUNK, stride=SUBLANES), :] = res[CHUNK:]


def _scan_prepare(ix, r_ref, a_ref, w_ref, iw_ref, k_ref, b_ref,
                  at_ref, rt_ref, bt_ref, kt_ref, gc_ref, lhs_ref, m_ref, ones_bd, between):
    g = None
    ig = None
    for tau in range(CHUNK):
        sl = slice(ix.loc(tau), ix.loc(tau) + SUBLANES)
        a = ix.rows(a_ref, tau)
        at_ref[sl, :] = a if g is None else a * g
        w = ix.rows(w_ref, tau)
        iw = ix.rows(iw_ref, tau)
        g = w if g is None else g * w
        ig = iw if ig is None else ig * iw
        rt_ref[sl, :] = ix.rows(r_ref, tau) * g
        bt_ref[sl, :] = ix.rows(b_ref, tau) * ig
        kt_ref[sl, :] = ix.rows(k_ref, tau) * ig
    gc_ref[...] = g
    between()

    srcs = {"a": at_ref, "r": rt_ref, "b": bt_ref, "k": kt_ref}

    def prod(n):
        kind, t, s = _PRODS[n]
        return ix.local(srcs[kind[0]], t) * ix.local(srcs[kind[1]], s)

    for q in range(len(_PRODS) // 4):
        left = jnp.concatenate([prod(4 * q), prod(4 * q + 2)], axis=0)
        right = jnp.concatenate([prod(4 * q + 1), prod(4 * q + 3)], axis=0)
        lhs_ref[16 * q:16 * (q + 1), 0:LANES] = left.astype(BF16)
        lhs_ref[16 * q:16 * (q + 1), LANES:2 * LANES] = right.astype(BF16)
    for q in range(_N_PROD_ROWS // _PROD_TILE):
        sl = slice(q * _PROD_TILE, (q + 1) * _PROD_TILE)
        m_ref[sl, :] = jnp.dot(lhs_ref[sl, :], ones_bd, preferred_element_type=F32)


def _scan_advance(ix, v_ref, y_ref, s_ref, sb_ref, bt_ref, kt_ref, gc_ref, m_ref,
                  ut_ref, sa_ref, sr_ref, diag):
    loc, rows, local = ix.loc, ix.rows, ix.local

    def coef(kind, t, s):
        r0, c0 = _PROD_POS[(kind, t, s)]
        return m_ref[r0:r0 + SUBLANES, c0:c0 + LANES]

    us = []
    for t in range(CHUNK):
        acc = local(sa_ref, t)
        for s in range(t):
            acc = acc + coef("ak", t, s) * rows(v_ref, s)
        for s in range(t):
            acc = acc + coef("ab", t, s) * us[s]
        us.append(acc)
        ut_ref[loc(t):loc(t) + SUBLANES, :] = acc

    for t in range(CHUNK):
        acc = local(sr_ref, t)
        for s in range(t + 1):
            acc = acc + coef("rk", t, s) * rows(v_ref, s) + coef("rb", t, s) * us[s]
        y_ref[pl.ds(pl.multiple_of(ix.base + loc(t), SUBLANES), SUBLANES), :] = acc

    for p in range(PAIRS):
        gp = gc_ref[p:p + 1, :]
        bk = jnp.concatenate([_pair_rows(bt_ref, p), _pair_rows(kt_ref, p)], axis=0) * gp
        uv = jnp.concatenate([_pair_rows(ut_ref, p),
                              v_ref[pl.ds(ix.base + p, CHUNK, stride=SUBLANES), :]], axis=0)
        upd = lax.dot_general(uv, bk, _TN, preferred_element_type=F32)
        s_new = s_ref[p] * gp + jnp.where(diag, upd, 0.0)
        s_ref[p] = s_new
        sb_ref[p] = s_new.astype(BF16)


_SCAN_SCRATCH_PER_DIR = 12


def _scan_kernel(*refs, n_chunks):
    ins, outs, scr = refs[:14], refs[14:16], refs[16:]
    rf, vf, af, rb, vb, ab, wf, iwf, kf, bf, wb, iwb, kb, bb = ins
    dirs = ((False, (rf, af, wf, iwf, kf, bf), vf, outs[0], scr[:_SCAN_SCRATCH_PER_DIR]),
            (True, (rb, ab, wb, iwb, kb, bb), vb, outs[1], scr[_SCAN_SCRATCH_PER_DIR:]))

    @pl.when(pl.program_id(0) == 0)
    def _():
        for _, _, _, _, st in dirs:
            for ref in st[0:2]:
                ref[...] = jnp.zeros_like(ref)

    rr = lax.broadcasted_iota(jnp.int32, (2 * LANES, 2 * LANES), 0) >> 6
    cc = lax.broadcasted_iota(jnp.int32, (2 * LANES, 2 * LANES), 1) >> 6
    ones_bd = (rr == cc).astype(BF16)
    ri = lax.broadcasted_iota(jnp.int32, (LANES, LANES), 0) >> 6
    ci = lax.broadcasted_iota(jnp.int32, (LANES, LANES), 1) >> 6
    diag = ri == ci

    def body(c, carry):
        for rev, ops, v_ref, y_ref, st in dirs:
            s, sb, at, rt, bt, kt, gc, m, lhs, ut, sa, sr = st
            ix = _ChunkIndex(c, n_chunks, rev)
            _scan_prepare(ix, *ops, at, rt, bt, kt, gc, lhs, m, ones_bd,
                          between=lambda: _scan_apply_state(sb, at, rt, sa, sr))
            _scan_advance(ix, v_ref, y_ref, s, sb, bt, kt, gc, m, ut, sa, sr, diag)
        return carry

    lax.fori_loop(0, n_chunks, body, 0)


def _scan(r, v, a, w, iw, kd, bd, tb):
    t = r.shape[0] // SUBLANES
    nb = t // tb
    rows = tb * SUBLANES
    sf = pl.BlockSpec((rows, LANES), lambda i: (i, 0))
    sb = pl.BlockSpec((rows, LANES), lambda i: (nb - 1 - i, 0))
    df = pl.BlockSpec((None, rows, LANES), lambda i: (0, i, 0))
    db = pl.BlockSpec((None, rows, LANES), lambda i: (1, nb - 1 - i, 0))
    c16 = CHUNK * SUBLANES
    per_dir = ([pltpu.VMEM((PAIRS, LANES, LANES), F32), pltpu.VMEM((PAIRS, LANES, LANES), BF16)]
               + [pltpu.VMEM((c16, LANES), F32) for _ in range(4)]
               + [pltpu.VMEM((SUBLANES, LANES), F32),
                  pltpu.VMEM((_N_PROD_ROWS, 2 * LANES), F32),
                  pltpu.VMEM((_N_PROD_ROWS, 2 * LANES), BF16)]
               + [pltpu.VMEM((c16, LANES), F32) for _ in range(3)])
    assert len(per_dir) == _SCAN_SCRATCH_PER_DIR
    out = jax.ShapeDtypeStruct((t * SUBLANES, LANES), F32)
    return pl.pallas_call(
        functools.partial(_scan_kernel, n_chunks=tb // CHUNK),
        grid=(nb,),
        in_specs=[sf, sf, sf, sb, sb, sb, df, df, df, df, db, db, db, db],
        out_specs=[sf, sb],
        out_shape=[out, out],
        scratch_shapes=per_dir + per_dir,
        compiler_params=_cparams(("arbitrary",)),
        name="rwkv_scan",
    )(r, v, a, r, v, a, w, iw, kd, bd, w, iw, kd, bd)


def _top4(logits):
    lane = lax.broadcasted_iota(jnp.int32, logits.shape, 1).astype(F32)
    vals, idxs = [], []
    l = logits
    for _ in range(TOP_K):
        m = jnp.max(l, axis=-1, keepdims=True)
        idx = jnp.min(jnp.where(l == m, lane, float(LANES)), axis=-1, keepdims=True)
        vals.append(m)
        idxs.append(idx)
        l = jnp.where(lane == idx, -jnp.inf, l)
    es = [jnp.exp(vk - vals[0]) for vk in vals]
    tot = es[0] + es[1] + es[2] + es[3]
    top_i = jnp.zeros(logits.shape, F32)
    gates = jnp.zeros(logits.shape, F32)
    for k in range(TOP_K):
        top_i = jnp.where(lane == float(k), idxs[k], top_i)
        gates = jnp.where(lane == float(k), es[k] / tot, gates)
    return top_i.astype(jnp.int32), gates


def _post_kernel(yf_ref, yb_ref, bonus_ref, g_ref, c_ref, cp_ref, cn_ref, h_ref, wo_ref,
                 lg_ref, lb_ref, cw_ref, os_ref, n1g_ref, n1b_ref, wrh_ref, wrl_ref, br_ref,
                 h1_ref, ti_ref, gt_ref, ocat_ref):
    i = pl.program_id(0)
    first = i == 0
    last = i == pl.num_programs(0) - 1
    ones = _head_ones2()
    inv_n = 1.0 / HEAD
    tm = h_ref.shape[0]

    for s in range(RW // LANES):
        cs = slice(s * LANES, (s + 1) * LANES)
        pair = pl.ds(s, tm, stride=SUBLANES)
        y = yf_ref[pair, :] + yb_ref[pair, :]
        mu = _head_sum(y, ones) * inv_n
        yc = y - mu
        var = _head_sum(yc * yc, ones) * inv_n
        yn = yc * lax.rsqrt(var + GN_EPS) * lg_ref[:, cs] + lb_ref[:, cs] + bonus_ref[:, cs]
        ocat_ref[:, cs] = (yn * g_ref[:, cs] * os_ref[:, cs]).astype(BF16)

    for s in range(CONV // LANES):
        cs = slice(s * LANES, (s + 1) * LANES)
        gs = slice(CONV + s * LANES, CONV + (s + 1) * LANES)
        hs = slice(2 * CONV + s * LANES, 2 * CONV + (s + 1) * LANES)
        u = c_ref[:, gs] * c_ref[:, hs]
        u_prev = jnp.where(first, 0.0, cp_ref[SUBLANES - 1:SUBLANES, gs] * cp_ref[SUBLANES - 1:SUBLANES, hs])
        u_next = jnp.where(last, 0.0, cn_ref[0:1, gs] * cn_ref[0:1, hs])
        up, un = _shift_rows(u, u_prev, u_next)
        conv = cw_ref[0:1, cs] * up + cw_ref[1:2, cs] * u + cw_ref[2:3, cs] * un
        oc = slice(RW + s * LANES, RW + (s + 1) * LANES)
        ocat_ref[:, oc] = (c_ref[:, cs] * conv * os_ref[:, oc]).astype(BF16)

    m = jnp.dot(ocat_ref[...], wo_ref[...], preferred_element_type=F32)
    h1 = _layer_norm(ALPHA_RES * h_ref[...] + m, n1g_ref[...], n1b_ref[...])
    h1_ref[...] = h1
    h_hi, h_lo = _split_bf16(h1)
    logits = (jnp.dot(h_hi, wrh_ref[...], preferred_element_type=F32)
              + jnp.dot(h_lo, wrh_ref[...], preferred_element_type=F32)
              + jnp.dot(h_hi, wrl_ref[...], preferred_element_type=F32)) + br_ref[...]
    ti, gt = _top4(logits)
    ti_ref[...] = ti
    gt_ref[...] = gt


def _post(yf, yb, bonus, g, pc, h, w_out_b, lnx_g, lnx_b, conv_w, out_scale, ln1_g, ln1_b,
          w_router, b_router, tm):
    t = h.shape[0]
    nblk = t // tm
    main, prev, nxt = _halo_specs(tm, CONV_COLS, nblk)

    def full(shape):
        return pl.BlockSpec(shape, lambda i: (0,) * len(shape))

    row = lambda c: pl.BlockSpec((tm, c), lambda i: (i, 0))
    srow = pl.BlockSpec((tm * SUBLANES, LANES), lambda i: (i, 0))
    wr = jnp.zeros((D_MODEL, LANES), F32).at[:, :N_EXP].set(w_router)
    wr_hi, wr_lo = _split_bf16(wr)
    br = jnp.full((1, LANES), -1e30, F32).at[0, :N_EXP].set(b_router)
    return pl.pallas_call(
        _post_kernel,
        grid=(nblk,),
        in_specs=[srow, srow, row(RW), row(RW),
                  main, prev, nxt, row(D_MODEL), full((D_MODEL, D_MODEL)),
                  full((1, RW)), full((1, RW)), full((3, CONV)), full((1, D_MODEL)),
                  full((1, D_MODEL)), full((1, D_MODEL)), full((D_MODEL, LANES)), full((D_MODEL, LANES)),
                  full((1, LANES))],
        out_specs=[row(D_MODEL), row(LANES), row(LANES)],
        out_shape=[jax.ShapeDtypeStruct((t, D_MODEL), F32),
                   jax.ShapeDtypeStruct((t, LANES), jnp.int32), jax.ShapeDtypeStruct((t, LANES), F32)],
        scratch_shapes=[pltpu.VMEM((tm, D_MODEL), BF16)],
        compiler_params=_cparams(("parallel",)),
        name="mixer_out",
    )(yf, yb, bonus, g, pc, pc, pc, h, w_out_b, lnx_g.reshape(1, RW), lnx_b.reshape(1, RW), conv_w,
      out_scale.reshape(1, D_MODEL), ln1_g.reshape(1, D_MODEL), ln1_b.reshape(1, D_MODEL), wr_hi, wr_lo, br)


def _moe_kernel(ve_ref, vr_ref, vn_ref, tail_ref, tok_hbm, x_hbm, wg_ref, wl_ref, bg_ref, bl_ref,
                wo_ref, bo_ref, y_hbm, xbuf, acc, wg_b, wl_b, wo_b, tok_smem, sem_tok, sem_in, sem_out):
    v = pl.program_id(0)
    f = pl.program_id(1)
    last_v = pl.num_programs(0) - 1
    last_f = pl.num_programs(1) - 1
    n128 = vn_ref[v]
    row0 = pl.multiple_of(vr_ref[v], MOE_BLOCK)
    nxt = jnp.minimum(v + 1, last_v)
    n128_next = jnp.where(v < last_v, vn_ref[nxt], 0)
    prv = jnp.maximum(v - 1, 0)
    n128_prev = jnp.where(v > 0, vn_ref[prv], 0)
    groups_per_block = MOE_BLOCK // SUBLANES

    def in_copy(j):
        tok = tok_smem[j >> 7, j & (MOE_BLOCK - 1)]
        return pltpu.make_async_copy(x_hbm.at[pl.ds(tok, 1), :], xbuf.at[pl.ds(j, 1), :], sem_in)

    def for_rows(n_groups, fn):
        def group(gi, c):
            for u in range(SUBLANES):
                fn(gi * SUBLANES + u)
            return c
        lax.fori_loop(0, n_groups, group, 0)

    def start_gather(first_row, nblk):
        tok_copy = pltpu.make_async_copy(
            tok_hbm.at[pl.ds(first_row // MOE_BLOCK, MOE_TMX // MOE_BLOCK), :], tok_smem, sem_tok)
        tok_copy.start()
        tok_copy.wait()
        for_rows(nblk * groups_per_block, lambda j: in_copy(j).start())

    def out_copy(j, first_row):
        r = pl.multiple_of(j * MOE_BLOCK, MOE_BLOCK)
        return pltpu.make_async_copy(acc.at[pl.ds(r, MOE_BLOCK), :],
                                     y_hbm.at[pl.ds(first_row + r, MOE_BLOCK), :], sem_out)

    @pl.when(jnp.logical_and(jnp.logical_and(v == 0, f == 0), n128 > 0))
    def _():
        start_gather(row0, n128)

    @pl.when(f == 0)
    def _():
        prev_row0 = pl.multiple_of(vr_ref[prv], MOE_BLOCK)
        lax.fori_loop(0, n128_prev, lambda j, c: (out_copy(j, prev_row0).wait(), c)[1], 0)

    @pl.when(jnp.logical_and(f == 0, n128 > 0))
    def _():
        def init(j, c):
            rs = pl.ds(pl.multiple_of(j * MOE_BLOCK, MOE_BLOCK), MOE_BLOCK)
            acc[rs, :] = jnp.broadcast_to(bo_ref[...], (MOE_BLOCK, D_MODEL))
            return c

        lax.fori_loop(0, n128, init, 0)
        for_rows(n128 * groups_per_block, lambda j: in_copy(j).wait())

    @pl.when(n128 > 0)
    def _():
        wg_b[...] = wg_ref[...].astype(BF16)
        wl_b[...] = wl_ref[...].astype(BF16)
        wo_b[...] = wo_ref[...].astype(BF16)

        def ffn(r0, m):
            rs = pl.ds(pl.multiple_of(r0, MOE_BLOCK), m)
            x = xbuf[rs, :].astype(BF16)
            hg = jnp.dot(x, wg_b[...], preferred_element_type=F32) + bg_ref[...]
            hl = jnp.dot(x, wl_b[...], preferred_element_type=F32) + bl_ref[...]
            glu = jnp.minimum(hg, SWIGLU_LIMIT)
            lin = jnp.clip(hl, -SWIGLU_LIMIT, SWIGLU_LIMIT)
            act = glu * jax.nn.sigmoid(SWIGLU_ALPHA * glu) * (lin + 1.0)
            acc[rs, :] += jnp.dot(act.astype(BF16), wo_b[...], preferred_element_type=F32)

        def pair(i, c):
            ffn(i * (2 * MOE_SUB), MOE_SUB)
            ffn(i * (2 * MOE_SUB) + MOE_SUB, MOE_SUB)
            return c

        n_pairs = n128 >> 2
        lax.fori_loop(0, n_pairs, pair, 0)
        rem = n128 & 3
        base = n_pairs * (2 * MOE_SUB)

        @pl.when(rem >= 2)
        def _():
            ffn(base, MOE_SUB)

        @pl.when((rem & 1) == 1)
        def _():
            ffn(base + (rem >> 1) * MOE_SUB, MOE_BLOCK)

    @pl.when(jnp.logical_and(f == last_f, n128 > 0))
    def _():
        lax.fori_loop(0, n128, lambda j, c: (out_copy(j, row0).start(), c)[1], 0)

    @pl.when(jnp.logical_and(f == last_f, n128_next > 0))
    def _():
        start_gather(pl.multiple_of(vr_ref[nxt], MOE_BLOCK), n128_next)

    @pl.when(jnp.logical_and(v == last_v, f == last_f))
    def _():
        lax.fori_loop(0, n128, lambda j, c: (out_copy(j, row0).wait(), c)[1], 0)
        acc[0:MOE_BLOCK, :] = jnp.zeros((MOE_BLOCK, D_MODEL), F32)

        def tail_copy(j):
            r = pl.multiple_of(tail_ref[0] + j * MOE_BLOCK, MOE_BLOCK)
            return pltpu.make_async_copy(acc.at[0:MOE_BLOCK, :], y_hbm.at[pl.ds(r, MOE_BLOCK), :], sem_out)

        lax.fori_loop(0, tail_ref[1], lambda j, c: (tail_copy(j).start(), c)[1], 0)
        lax.fori_loop(0, tail_ref[1], lambda j, c: (tail_copy(j).wait(), c)[1], 0)


def _moe_experts(h, vis_tok, vis_e, vis_row, vis_n, tail, layer, w_in, b_in, w_out, b_out, p_rows):
    nv = vis_e.shape[0]
    nf = D_EXP // MOE_TF

    def tile(v, f, vn):
        return jnp.where(vn[v] > 0, f, nf - 1)

    grid_spec = pltpu.PrefetchScalarGridSpec(
        num_scalar_prefetch=4,
        grid=(nv, nf),
        in_specs=[
            pl.BlockSpec(memory_space=pl.ANY),
            pl.BlockSpec(memory_space=pl.ANY),
            pl.BlockSpec((None, None, D_MODEL, MOE_TF), lambda v, f, ve, vr, vn, tl: (layer, ve[v], 0, tile(v, f, vn))),
            pl.BlockSpec((None, None, D_MODEL, MOE_TF), lambda v, f, ve, vr, vn, tl: (layer, ve[v], 0, nf + tile(v, f, vn))),
            pl.BlockSpec((None, None, 1, MOE_TF), lambda v, f, ve, vr, vn, tl: (layer, ve[v], 0, tile(v, f, vn))),
            pl.BlockSpec((None, None, 1, MOE_TF), lambda v, f, ve, vr, vn, tl: (layer, ve[v], 0, nf + tile(v, f, vn))),
            pl.BlockSpec((None, None, MOE_TF, D_MODEL), lambda v, f, ve, vr, vn, tl: (layer, ve[v], tile(v, f, vn), 0)),
            pl.BlockSpec((None, None, 1, D_MODEL), lambda v, f, ve, vr, vn, tl: (layer, ve[v], 0, 0)),
        ],
        out_specs=pl.BlockSpec(memory_space=pl.ANY),
        scratch_shapes=[pltpu.VMEM((MOE_TMX, D_MODEL), F32), pltpu.VMEM((MOE_TMX, D_MODEL), F32),
                        pltpu.VMEM((D_MODEL, MOE_TF), BF16), pltpu.VMEM((D_MODEL, MOE_TF), BF16),
                        pltpu.VMEM((MOE_TF, D_MODEL), BF16),
                        pltpu.SMEM((MOE_TMX // MOE_BLOCK, MOE_BLOCK), jnp.int32),
                        pltpu.SemaphoreType.DMA(()), pltpu.SemaphoreType.DMA(()),
                        pltpu.SemaphoreType.DMA(())],
    )
    return pl.pallas_call(
        _moe_kernel,
        grid_spec=grid_spec,
        out_shape=jax.ShapeDtypeStruct((p_rows, D_MODEL), F32),
        compiler_params=_cparams(("arbitrary", "arbitrary")),
        name="moe_experts",
    )(vis_e, vis_row, vis_n, tail, vis_tok, h, w_in, w_in, b_in.reshape(DEPTH, N_EXP, 1, 2 * D_EXP),
      b_in.reshape(DEPTH, N_EXP, 1, 2 * D_EXP), w_out, b_out.reshape(DEPTH, N_EXP, 1, D_MODEL))


def _route(top_i, t):
    a = t * TOP_K
    flat_e = top_i.reshape(-1)
    onehot = (flat_e[:, None] == jnp.arange(N_EXP, dtype=jnp.int32)[None, :]).astype(jnp.int32)
    csum = jnp.cumsum(onehot, axis=0)
    rank = jnp.take_along_axis(csum, flat_e[:, None], axis=1)[:, 0] - 1
    counts = csum[-1]
    padded = (counts + MOE_BLOCK - 1) // MOE_BLOCK * MOE_BLOCK
    pends = jnp.cumsum(padded)
    pstarts = pends - padded
    dest = pstarts[flat_e] + rank

    p_rows = (a + N_EXP * (MOE_BLOCK - 1) + MOE_BLOCK - 1) // MOE_BLOCK * MOE_BLOCK
    p_alloc = p_rows + MOE_TMX
    row_tok = jnp.zeros((p_alloc,), jnp.int32).at[dest].set(jnp.arange(a, dtype=jnp.int32) // TOP_K)

    n_vis = N_EXP + (p_rows + MOE_TMX - 1) // MOE_TMX
    per_e = (padded + MOE_TMX - 1) // MOE_TMX
    vend = jnp.cumsum(per_e)
    vstart = vend - per_e
    vid = jnp.arange(n_vis, dtype=jnp.int32)
    valid = vid < vend[-1]
    e_of = jnp.clip(jnp.searchsorted(vend, vid, side="right"), 0, N_EXP - 1).astype(jnp.int32)
    e_last = jnp.clip(jnp.searchsorted(vend, vend[-1] - 1, side="right"), 0, N_EXP - 1).astype(jnp.int32)
    e_of = jnp.where(valid, e_of, e_last)
    local = vid - vstart[e_of]
    vis_row = jnp.where(valid, pstarts[e_of] + local * MOE_TMX, 0).astype(jnp.int32)
    vis_rows = jnp.where(valid, jnp.clip(padded[e_of] - local * MOE_TMX, 0, MOE_TMX), 0)
    vis_n = (vis_rows // MOE_BLOCK).astype(jnp.int32)
    tail = jnp.stack([pends[-1], (p_rows - pends[-1]) // MOE_BLOCK]).astype(jnp.int32)
    return dest, row_tok.reshape(p_alloc // MOE_BLOCK, MOE_BLOCK), e_of, vis_row, vis_n, tail, p_rows


def _combine_kernel(pos_ref, y_hbm, gt_ref, h_ref, g_ref, b_ref, o_ref, ob_ref, buf, sem):
    tm = h_ref.shape[0]

    def copy(t, k):
        return pltpu.make_async_copy(y_hbm.at[pl.ds(pos_ref[0, t * TOP_K + k], 1), :],
                                     buf.at[k, pl.ds(t, 1), :], sem)

    def for_tokens(fn):
        def group(gi, c):
            for u in range(SUBLANES):
                for k in range(TOP_K):
                    fn(gi * SUBLANES + u, k)
            return c
        lax.fori_loop(0, tm // SUBLANES, group, 0)

    for_tokens(lambda t, k: copy(t, k).start())
    for_tokens(lambda t, k: copy(t, k).wait())
    gt = gt_ref[...]
    f = (gt[:, 0:1] * buf[0] + gt[:, 1:2] * buf[1] + gt[:, 2:3] * buf[2] + gt[:, 3:4] * buf[3])
    h2 = _layer_norm(ALPHA_RES * h_ref[...] + f, g_ref[...], b_ref[...])
    o_ref[...] = h2
    ob_ref[...] = h2.astype(BF16)


def _combine(yb, pos, gates, h, g, b, tm):
    t, d = h.shape
    nt = t // tm
    row = pl.BlockSpec((tm, d), lambda i: (i, 0))
    vec = pl.BlockSpec((1, d), lambda i: (0, 0))
    return pl.pallas_call(
        _combine_kernel,
        grid=(nt,),
        in_specs=[pl.BlockSpec((None, 1, tm * TOP_K), lambda i: (i, 0, 0), memory_space=pltpu.SMEM),
                  pl.BlockSpec(memory_space=pl.ANY),
                  pl.BlockSpec((tm, LANES), lambda i: (i, 0)), row, vec, vec],
        out_specs=[row, row],
        out_shape=[jax.ShapeDtypeStruct((t, d), F32), jax.ShapeDtypeStruct((t, d), BF16)],
        scratch_shapes=[pltpu.VMEM((TOP_K, tm, d), F32), pltpu.SemaphoreType.DMA(())],
        compiler_params=_cparams(("arbitrary",)),
        name="moe_combine",
    )(pos.reshape(nt, 1, tm * TOP_K), yb, gates, h, g.reshape(1, d), b.reshape(1, d))


def kernel(x, meta_tokens, ln_in_g, ln_in_b, w_in, mu_shift, decay0, decay_up, iclr0, iclr_up,
           gate_up, k_k, k_a, r_k, lnx_g, lnx_b, conv_w, out_scale, w_out, ln1_g, ln1_b,
           w_router, b_router, w_exp_in, b_exp_in, w_exp_out, b_exp_out, ln2_g, ln2_b):
    bsz, seq, d = x.shape
    assert bsz == 1 and d == D_MODEL
    t = N_META + seq
    tm = _largest_divisor(t, 16, 512)
    tm_post = _largest_divisor(t, 16, 320)
    tb = _largest_divisor(t, CHUNK, 160)

    h0 = jnp.concatenate([meta_tokens.astype(x.dtype), x[0]], axis=0)
    h, hb = _ln_in(h0, ln_in_g, ln_in_b, tm)
    for l in range(DEPTH):
        w_in_b = w_in[l].astype(BF16)
        pr = _matmul(hb, w_in_b[:, :RW_COLS], tm, RW_COLS // 3)
        pc = _matmul(hb, w_in_b[:, RW_COLS:], tm, CONV_COLS // 3)
        r, v, a, w, iw, kd, bd, bonus, g = _prep(pr, mu_shift[l], decay0[l], decay_up[l], iclr0[l],
                                                 iclr_up[l], gate_up[l], k_k[l], k_a[l], r_k[l], tb)
        yf, yb_ = _scan(r, v, a, w, iw, kd, bd, tb)
        h1, top_i, gates = _post(yf, yb_, bonus, g, pc, h, w_out[l].astype(BF16), lnx_g[l], lnx_b[l],
                                      conv_w[l], out_scale[l], ln1_g[l], ln1_b[l],
                                      w_router[l], b_router[l], tm_post)
        pos, vis_tok, vis_e, vis_row, vis_n, tail, p_rows = _route(top_i[:, :TOP_K], t)
        yb = _moe_experts(h1, vis_tok, vis_e, vis_row, vis_n, tail, l, w_exp_in, b_exp_in, w_exp_out,
                          b_exp_out, p_rows)
        h, hb = _combine(yb, pos, gates, h1, ln2_g[l], ln2_b[l], tm_post)
    return h[N_META:].reshape(bsz, seq, d)
```
